```python
import jax, jax.numpy as jnp
from jax import lax
import numpy as np

D_MODEL = 1024
BATCH = 16
SEQ = 4096
DEPTH = 4

EPS = 1e-6
M_HEADS = 4
M_WIDTH = D_MODEL
M_V_DIM = M_WIDTH // M_HEADS
M_QK_DIM = M_V_DIM // 2
M_QK_WIDTH = M_HEADS * M_QK_DIM
M_CHUNK = 64
CONV_K = 4
A_HEAD_DIM = 64
A_WIDTH = D_MODEL // 2
A_HEADS = A_WIDTH // A_HEAD_DIM
DILATED_PATTERNS = ((128, 1), (512, 4), (2048, 16))
A_BLOCK = 128
D_MIX = M_WIDTH + A_WIDTH
IN_WIDTHS = (M_QK_WIDTH, M_QK_WIDTH, M_WIDTH, M_WIDTH, M_WIDTH, M_HEADS, M_HEADS,
             A_WIDTH, A_WIDTH, A_WIDTH, A_WIDTH)
N_IN = sum(IN_WIDTHS)

kernel_name = "hymba_mlstm_dilated_swa_trunk"


def rms_norm(x, g):
    xf = x.astype(jnp.float32)
    y = xf * lax.rsqrt(jnp.mean(xf * xf, axis=-1, keepdims=True) + EPS)
    return (y * g.astype(jnp.float32)).astype(x.dtype)


def causal_depthwise_conv(x, w, b):
    C = x.shape[-1]
    out = lax.conv_general_dilated(
        x, w[:, None, :], window_strides=(1,), padding=((CONV_K - 1, 0),),
        dimension_numbers=("NWC", "WIO", "NWC"), feature_group_count=C)
    return out + b


def mlstm_chunkwise(q, k, v, i_pre, f_pre):
    B, S, H, dk = q.shape
    dv = v.shape[-1]
    L = M_CHUNK
    nc = S // L
    f32 = jnp.float32

    def to_chunks(t):
        t = t.reshape((B, nc, L, H) + t.shape[3:])
        return jnp.moveaxis(t, (1, 3), (0, 2))

    qc = to_chunks(q.astype(f32) * (dk ** -0.5))
    kc = to_chunks(k.astype(f32))
    vc = to_chunks(v.astype(f32))
    ic = to_chunks(i_pre.astype(f32))
    lfc = to_chunks(jax.nn.log_sigmoid(f_pre.astype(f32)))
    causal = jnp.tril(jnp.ones((L, L), dtype=bool))

    def step(carry, xs):
        C, n, m = carry
        qb, kb, vb, ib, lfb = xs
        b = jnp.cumsum(lfb, axis=-1)
        log_intra = jnp.where(causal, b[..., :, None] - b[..., None, :] + ib[..., None, :], -jnp.inf)
        log_inter = b + m[..., None]
        m_t = jnp.maximum(log_inter, jnp.max(log_intra, axis=-1))
        w_intra = jnp.exp(log_intra - m_t[..., None])
        w_inter = jnp.exp(log_inter - m_t)
        s = jnp.einsum('bhtd,bhsd->bhts', qb, kb) * w_intra
        num = jnp.einsum('bhts,bhsv->bhtv', s, vb) + w_inter[..., None] * jnp.einsum('bhtd,bhdv->bhtv', qb, C)
        den = jnp.sum(s, axis=-1) + w_inter * jnp.einsum('bhtd,bhd->bht', qb, n)
        h = num / jnp.maximum(jnp.abs(den), jnp.exp(-m_t))[..., None]
        b_last = b[..., -1]
        log_state = b_last[..., None] - b + ib
        m_new = jnp.maximum(b_last + m, jnp.max(log_state, axis=-1))
        w_s = jnp.exp(log_state - m_new[..., None])
        decay = jnp.exp(b_last + m - m_new)
        C_new = decay[..., None, None] * C + jnp.einsum('bhs,bhsd,bhsv->bhdv', w_s, kb, vb)
        n_new = decay[..., None] * n + jnp.einsum('bhs,bhsd->bhd', w_s, kb)
        return (C_new, n_new, m_new), h

    init = (jnp.zeros((B, H, dk, dv), f32), jnp.zeros((B, H, dk), f32), jnp.zeros((B, H), f32))
    _, h = lax.scan(step, init, (qc, kc, vc, ic, lfc))
    return jnp.moveaxis(h, (0, 2), (1, 3)).reshape(B, S, H, dv)


def dilated_window_attention(q, k, v, window, dilation):
    B, H, S, hd = q.shape
    L = S // dilation
    J = window // dilation
    nb = -(-L // A_BLOCK)
    Lp = nb * A_BLOCK

    def to_sub(t):
        return jnp.swapaxes(t.reshape(B, H, L, dilation, hd), 2, 3)

    qs = jnp.pad(to_sub(q), ((0, 0), (0, 0), (0, 0), (0, Lp - L), (0, 0)))
    pad_kv = ((0, 0), (0, 0), (0, 0), (A_BLOCK, Lp - L), (0, 0))
    ks = jnp.pad(to_sub(k), pad_kv)
    vs = jnp.pad(to_sub(v), pad_kv)
    qb = qs.reshape(B, H, dilation, nb, A_BLOCK, hd)

    def kv_blocks(t):
        prev = t[..., :Lp, :].reshape(B, H, dilation, nb, A_BLOCK, hd)
        cur = t[..., A_BLOCK:, :].reshape(B, H, dilation, nb, A_BLOCK, hd)
        return jnp.concatenate([prev, cur], axis=-2)

    kb = kv_blocks(ks)
    vb = kv_blocks(vs)
    blk = jnp.arange(nb)[:, None, None]
    qi = jnp.arange(A_BLOCK)[None, :, None]
    kc = jnp.arange(2 * A_BLOCK)[None, None, :]
    delta = qi - kc + A_BLOCK
    k_pos = blk * A_BLOCK - A_BLOCK + kc
    mask = (delta >= 0) & (delta <= J) & (k_pos >= 0)

    s = jnp.einsum('bhrnqd,bhrnkd->bhrnqk', qb, kb, preferred_element_type=jnp.float32)
    s = jnp.where(mask, s, -jnp.inf)
    mx = jnp.max(s, axis=-1, keepdims=True)
    p = jnp.exp(s - mx)
    den = jnp.sum(p, axis=-1, keepdims=True)
    o = jnp.einsum('bhrnqk,bhrnkd->bhrnqd', p, vb.astype(jnp.float32)) / den
    lse = (mx + jnp.log(den))[..., 0]
    o = o.reshape(B, H, dilation, Lp, hd)[..., :L, :]
    o = jnp.swapaxes(o, 2, 3).reshape(B, H, S, hd)
    lse = lse.reshape(B, H, dilation, Lp)[..., :L]
    lse = jnp.swapaxes(lse, 2, 3).reshape(B, H, S)
    return o, lse


def hybrid_layer(x, norm_g, w_in, gate_b, conv_w, conv_b, m_norm_g, q_norm_g, k_norm_g, w_out):
    B, S, _ = x.shape
    h = rms_norm(x, norm_g)
    proj = jnp.einsum('bsd,dn->bsn', h, w_in)
    split_points = np.cumsum(IN_WIDTHS)[:-1]
    mq, mk, mv, mo, mz, mi, mf, aq, ak, av, az = jnp.split(proj, split_points, axis=-1)

    qk = jax.nn.silu(causal_depthwise_conv(jnp.concatenate([mq, mk], axis=-1), conv_w, conv_b))
    mq, mk = jnp.split(qk, 2, axis=-1)
    gates = jnp.concatenate([mi, mf], axis=-1) + gate_b
    gi, gf = jnp.split(gates, 2, axis=-1)
    hm = mlstm_chunkwise(mq.reshape(B, S, M_HEADS, M_QK_DIM), mk.reshape(B, S, M_HEADS, M_QK_DIM),
                         mv.reshape(B, S, M_HEADS, M_V_DIM), gi, gf)
    hm = jax.nn.sigmoid(mo.astype(jnp.float32)).reshape(B, S, M_HEADS, M_V_DIM) * hm
    hm = rms_norm(hm, m_norm_g.reshape(M_HEADS, M_V_DIM)).reshape(B, S, M_WIDTH)
    hm = hm.astype(x.dtype) * jax.nn.silu(mz)

    def heads(t, g):
        t = t.reshape(B, S, A_HEADS, A_HEAD_DIM)
        if g is not None:
            t = rms_norm(t, g)
        return jnp.transpose(t, (0, 2, 1, 3))
    qa = heads(aq, q_norm_g) * (A_HEAD_DIM ** -0.5)
    ka = heads(ak, k_norm_g)
    va = heads(av, None)
    outs, lses = [], []
    for window, dilation in DILATED_PATTERNS:
        o, lse = dilated_window_attention(qa, ka, va, window, dilation)
        outs.append(o)
        lses.append(lse)
    wts = jax.nn.softmax(jnp.stack(lses, axis=0), axis=0)
    ha = jnp.sum(wts[..., None] * jnp.stack(outs, axis=0), axis=0)
    ha = jnp.transpose(ha, (0, 2, 1, 3)).reshape(B, S, A_WIDTH).astype(x.dtype) * jax.nn.silu(az)

    y = jnp.einsum('bsm,md->bsd', jnp.concatenate([hm, ha], axis=-1), w_out)
    return x + y.astype(x.dtype)


def setup_inputs(seed: int = 0) -> dict:
    key = jax.random.key(seed)
    ks = jax.random.split(key, 12)
    f32 = jnp.float32
    x = jax.random.normal(ks[0], (BATCH, SEQ, D_MODEL), f32)
    norm_g = 1.0 + 0.02 * jax.random.normal(ks[1], (DEPTH, D_MODEL), f32)
    w_in = jax.random.normal(ks[2], (DEPTH, D_MODEL, N_IN), f32) * (D_MODEL ** -0.5)
    i_bias = 0.1 * jax.random.normal(ks[3], (DEPTH, M_HEADS), f32)
    f_bias = jnp.linspace(3.0, 6.0, M_HEADS, dtype=f32)[None, :] + 0.01 * jax.random.normal(ks[4], (DEPTH, M_HEADS), f32)
    gate_b = jnp.concatenate([i_bias, f_bias], axis=-1)
    conv_w = jax.random.normal(ks[5], (DEPTH, CONV_K, 2 * M_QK_WIDTH), f32) * (CONV_K ** -0.5)
    conv_b = 0.01 * jax.random.normal(ks[6], (DEPTH, 2 * M_QK_WIDTH), f32)
    m_norm_g = 1.0 + 0.02 * jax.random.normal(ks[7], (DEPTH, M_WIDTH), f32)
    q_norm_g = 1.0 + 0.02 * jax.random.normal(ks[8], (DEPTH, A_HEAD_DIM), f32)
    k_norm_g = 1.0 + 0.02 * jax.random.normal(ks[9], (DEPTH, A_HEAD_DIM), f32)
    w_out = jax.random.normal(ks[10], (DEPTH, D_MIX, D_MODEL), f32) * (D_MIX ** -0.5)
    return {"x": x, "norm_g": norm_g, "w_in": w_in, "gate_b": gate_b, "conv_w": conv_w,
            "conv_b": conv_b, "m_norm_g": m_norm_g, "q_norm_g": q_norm_g, "k_norm_g": k_norm_g,
            "w_out": w_out}


def reference(x, norm_g, w_in, gate_b, conv_w, conv_b, m_norm_g, q_norm_g, k_norm_g, w_out):
    for layer in range(DEPTH):
        x = hybrid_layer(x, norm_g[layer], w_in[layer], gate_b[layer], conv_w[layer], conv_b[layer],
                         m_norm_g[layer], q_norm_g[layer], k_norm_g[layer], w_out[layer])
    return x
```

```python
import functools

import jax
import jax.numpy as jnp
from jax import lax
from jax.experimental import pallas as pl
from jax.experimental.pallas import tpu as pltpu

F32 = jnp.float32
BF16 = jnp.bfloat16

EPS = 1e-6
D_MODEL = 1024
M_HEADS = 4
M_V_DIM = 256
M_QK_DIM = 128
M_QK_WIDTH = M_HEADS * M_QK_DIM
M_WIDTH = M_HEADS * M_V_DIM
CONV_K = 4
A_HEAD_DIM = 64
A_HEADS = 8
A_WIDTH = A_HEADS * A_HEAD_DIM
A_BLOCK = 128
DILATED_PATTERNS = ((128, 1), (512, 4), (2048, 16))
D_MIX = M_WIDTH + A_WIDTH

LANES = 128
GATE_PAD = LANES
N_MAIN = 2 * M_QK_WIDTH + 3 * M_WIDTH + 4 * A_WIDTH
NEG_BIG = -1e30

V7X_VMEM_BYTES = 64 * 1024 * 1024
VMEM_LIMIT = 56 * 1024 * 1024

TM_PROJ = 512
M_CHUNK = 256


def _sigmoid(x):
    return 1.0 / (1.0 + jnp.exp(-x))


def _split3(x):
    hi = x.astype(BF16)
    r1 = x - hi.astype(F32)
    mid = r1.astype(BF16)
    lo = (r1 - mid.astype(F32)).astype(BF16)
    return hi, mid, lo


def _dot(a, b):
    return jnp.dot(a, b, preferred_element_type=F32)


def _dot_exact_rhs(x, m):
    hi, mid, lo = _split3(x)
    return _dot(hi, m) + _dot(mid, m) + _dot(lo, m)


def _dot_exact_lhs(m, x):
    hi, mid, lo = _split3(x)
    return _dot(m, hi) + _dot(m, mid) + _dot(m, lo)


def _inproj_kernel(x_ref, g_ref, w_ref, bd_ref, gq_ref, gk_ref,
                   qk_ref, v_ref, o_ref, z_ref, aq_ref, ak_ref, av_ref, az_ref, gate_ref):
    x = x_ref[...]
    ms = jnp.mean(x * x, axis=-1, keepdims=True)
    h = (x * lax.rsqrt(ms + EPS) * g_ref[...]).astype(BF16)

    def proj(c0, c1):
        return _dot(h, w_ref[:, c0:c1])

    def head_norm(a, gain_ref):
        ss = _dot_exact_rhs(a * a, bd_ref[...])
        return a * lax.rsqrt(ss * (1.0 / A_HEAD_DIM) + EPS) * gain_ref[...]

    c = 0
    qk_ref[...] = proj(c, c + 2 * M_QK_WIDTH).astype(BF16); c += 2 * M_QK_WIDTH
    v_ref[...] = proj(c, c + M_WIDTH).astype(BF16); c += M_WIDTH
    o_ref[...] = proj(c, c + M_WIDTH).astype(BF16); c += M_WIDTH
    z_ref[...] = proj(c, c + M_WIDTH).astype(BF16); c += M_WIDTH
    aq_ref[...] = head_norm(proj(c, c + A_WIDTH), gq_ref).astype(BF16); c += A_WIDTH
    ak_ref[...] = head_norm(proj(c, c + A_WIDTH), gk_ref).astype(BF16); c += A_WIDTH
    av_ref[...] = proj(c, c + A_WIDTH).astype(BF16); c += A_WIDTH
    az_ref[...] = proj(c, c + A_WIDTH).astype(BF16); c += A_WIDTH
    gate_ref[...] = proj(c, c + GATE_PAD)


def _inproj(x2d, norm_g, w_all, bd, gq, gk):
    T = x2d.shape[0]
    tm = TM_PROJ
    const = lambda i: (0, 0)
    row = lambda i: (i, 0)
    widths = (2 * M_QK_WIDTH, M_WIDTH, M_WIDTH, M_WIDTH, A_WIDTH, A_WIDTH, A_WIDTH, A_WIDTH)
    out_shape = [jax.ShapeDtypeStruct((T, w), BF16) for w in widths] + [jax.ShapeDtypeStruct((T, GATE_PAD), F32)]
    out_specs = [pl.BlockSpec((tm, w), row) for w in widths] + [pl.BlockSpec((tm, GATE_PAD), row)]
    return pl.pallas_call(
        _inproj_kernel,
        grid=(T // tm,),
        in_specs=[
            pl.BlockSpec((tm, D_MODEL), row),
            pl.BlockSpec((1, D_MODEL), const),
            pl.BlockSpec((D_MODEL, N_MAIN + GATE_PAD), const, pipeline_mode=pl.Buffered(1)),
            pl.BlockSpec((A_WIDTH, A_WIDTH), const),
            pl.BlockSpec((1, A_WIDTH), const),
            pl.BlockSpec((1, A_WIDTH), const),
        ],
        out_specs=out_specs,
        out_shape=out_shape,
        compiler_params=pltpu.CompilerParams(dimension_semantics=("arbitrary",), vmem_limit_bytes=VMEM_LIMIT),
        name="inproj",
    )(x2d, norm_g, w_all, bd, gq, gk)


def _mlstm_kernel(qk_ref, v_ref, o_ref, z_ref, gate_ref, gb_ref, cw_ref, cb_ref, mg_ref, tri_ref,
                  out_ref, cbuf, c_ref, n_ref, m_ref):
    L = qk_ref.shape[0]
    step = pl.program_id(1)

    @pl.when(step == 0)
    def _():
        cbuf[0:8, :] = jnp.zeros((8, cbuf.shape[1]), F32)
        c_ref[...] = jnp.zeros(c_ref.shape, F32)
        n_ref[...] = jnp.zeros(n_ref.shape, F32)
        m_ref[...] = jnp.zeros(m_ref.shape, F32)

    @pl.when(step > 0)
    def _():
        cbuf[0:8, :] = cbuf[L:L + 8, :]

    cbuf[8:8 + L, :] = qk_ref[...].astype(F32)
    conv = cb_ref[...] + cw_ref[0:1, :] * cbuf[5:5 + L, :]
    for j in range(1, CONV_K):
        conv = conv + cw_ref[j:j + 1, :] * cbuf[5 + j:5 + j + L, :]
    act = conv * _sigmoid(conv)

    g = gate_ref[...] + gb_ref[...]
    lane = lax.broadcasted_iota(jnp.int32, (L, LANES), 1)
    logf = jnp.minimum(g, 0.0) - jnp.log1p(jnp.exp(-jnp.abs(g)))
    bcum = _dot_exact_lhs(tri_ref[...], logf)
    gb = jnp.where(lane < M_HEADS, g, bcum)
    gbt = gb.T

    def col(idx):
        return jnp.sum(jnp.where(lane == idx, gb, 0.0), axis=1, keepdims=True)

    t_idx = lax.broadcasted_iota(jnp.int32, (L, L), 0)
    s_idx = lax.broadcasted_iota(jnp.int32, (L, L), 1)
    causal = s_idx <= t_idx
    last_row = lax.broadcasted_iota(jnp.int32, (L, 1), 0) == (L - 1)

    for h in range(M_HEADS):
        i_col, b_col = col(h), col(M_HEADS + h)
        i_row, b_row = gbt[h:h + 1, :], gbt[M_HEADS + h:M_HEADS + h + 1, :]
        m_prev = m_ref[h]
        c_prev = c_ref[h]
        n_prev = n_ref[h]

        q = (act[:, h * M_QK_DIM:(h + 1) * M_QK_DIM] * (M_QK_DIM ** -0.5)).astype(BF16)
        k_f = act[:, M_QK_WIDTH + h * M_QK_DIM:M_QK_WIDTH + (h + 1) * M_QK_DIM]
        k = k_f.astype(BF16)
        v = v_ref[:, h * M_V_DIM:(h + 1) * M_V_DIM]

        log_intra = jnp.where(causal, b_col - b_row + i_row, NEG_BIG)
        log_inter = b_col + m_prev
        m_t = jnp.maximum(log_inter, jnp.max(log_intra, axis=1, keepdims=True))
        w_intra = jnp.exp(log_intra - m_t)
        w_inter = jnp.exp(log_inter - m_t)

        s = lax.dot_general(q, k, (((1,), (1,)), ((), ())), preferred_element_type=F32) * w_intra
        num = _dot(s.astype(BF16), v) + w_inter * _dot(q, c_prev.astype(BF16))
        den = jnp.sum(s, axis=1, keepdims=True) + w_inter * jnp.sum(q.astype(F32) * n_prev, axis=1, keepdims=True)
        hid = num / jnp.maximum(jnp.abs(den), jnp.exp(-m_t))

        b_last = jnp.sum(jnp.where(last_row, b_col, 0.0), axis=0, keepdims=True)
        log_state = b_last - b_col + i_col
        m_new = jnp.maximum(b_last + m_prev, jnp.max(log_state, axis=0, keepdims=True))
        w_s = jnp.exp(log_state - m_new)
        decay = jnp.exp(b_last + m_prev - m_new)
        kw = k_f * w_s
        c_ref[h] = decay * c_prev + lax.dot_general(kw.astype(BF16), v, (((0,), (0,)), ((), ())),
                                                    preferred_element_type=F32)
        n_ref[h] = decay * n_prev + jnp.sum(kw, axis=0, keepdims=True)
        m_ref[h] = m_new

        sl = slice(h * M_V_DIM, (h + 1) * M_V_DIM)
        hg = _sigmoid(o_ref[:, sl].astype(F32)) * hid
        hn = hg * lax.rsqrt(jnp.mean(hg * hg, axis=-1, keepdims=True) + EPS) * mg_ref[:, sl]
        zz = z_ref[:, sl].astype(F32)
        out_ref[:, sl] = (hn * (zz * _sigmoid(zz))).astype(BF16)


def _mlstm(qk, v, o, z, gates, gate_b, conv_w, conv_b, m_norm_g, tri, B, S):
    L = M_CHUNK
    nc = S // L
    row = lambda b, s: (b * nc + s, 0)
    const = lambda b, s: (0, 0)
    return pl.pallas_call(
        _mlstm_kernel,
        grid=(B, nc),
        in_specs=[
            pl.BlockSpec((L, 2 * M_QK_WIDTH), row),
            pl.BlockSpec((L, M_WIDTH), row),
            pl.BlockSpec((L, M_WIDTH), row),
            pl.BlockSpec((L, M_WIDTH), row),
            pl.BlockSpec((L, GATE_PAD), row),
            pl.BlockSpec((1, GATE_PAD), const),
            pl.BlockSpec((CONV_K, 2 * M_QK_WIDTH), const),
            pl.BlockSpec((1, 2 * M_QK_WIDTH), const),
            pl.BlockSpec((1, M_WIDTH), const),
            pl.BlockSpec((L, L), const),
        ],
        out_specs=pl.BlockSpec((L, M_WIDTH), row),
        out_shape=jax.ShapeDtypeStruct((B * S, M_WIDTH), BF16),
        scratch_shapes=[
            pltpu.VMEM((L + 8, 2 * M_QK_WIDTH), F32),
            pltpu.VMEM((M_HEADS, M_QK_DIM, M_V_DIM), F32),
            pltpu.VMEM((M_HEADS, 1, M_QK_DIM), F32),
            pltpu.VMEM((M_HEADS, 1, 1), F32),
        ],
        compiler_params=pltpu.CompilerParams(dimension_semantics=("arbitrary", "arbitrary"),
                                             vmem_limit_bytes=VMEM_LIMIT),
        name="mlstm",
    )(qk, v, o, z, gates, gate_b, conv_w, conv_b, m_norm_g, tri)


def _attn_kernel(q_ref, kp_ref, kc_ref, vp_ref, vc_ref, o_ref, lse_ref, kbuf, vbuf):
    TQ = q_ref.shape[0]
    i = pl.program_id(2)
    kbuf[0:A_BLOCK, :] = kp_ref[...]
    kbuf[A_BLOCK:A_BLOCK + TQ, :] = kc_ref[...]
    vbuf[0:A_BLOCK, :] = vp_ref[...]
    vbuf[A_BLOCK:A_BLOCK + TQ, :] = vc_ref[...]

    qi = lax.broadcasted_iota(jnp.int32, (A_BLOCK, 2 * A_BLOCK), 0)
    kc = lax.broadcasted_iota(jnp.int32, (A_BLOCK, 2 * A_BLOCK), 1)
    band = (kc >= qi) & (kc <= qi + A_BLOCK)
    first_lim = jnp.where(i > 0, 0, A_BLOCK)
    band_first = band & (kc >= first_lim)
    lane = lax.broadcasted_iota(jnp.int32, (A_BLOCK, LANES), 1)
    low_half = lane < A_HEAD_DIM

    for j in range(TQ // A_BLOCK):
        mask = band_first if j == 0 else band
        rows = slice(j * A_BLOCK, (j + 1) * A_BLOCK)
        krows = slice(j * A_BLOCK, (j + 2) * A_BLOCK)
        lse_tile = jnp.zeros((A_BLOCK, LANES), F32)
        for p in range(A_HEADS // 2):
            cols = slice(p * LANES, (p + 1) * LANES)
            q2 = q_ref[rows, cols]
            kk = kbuf[krows, cols]
            vv = vbuf[krows, cols]
            outs = []
            for e in range(2):
                sel = low_half if e == 0 else jnp.logical_not(low_half)
                qm = jnp.where(sel, q2, jnp.zeros_like(q2))
                s = lax.dot_general(qm, kk, (((1,), (1,)), ((), ())), preferred_element_type=F32)
                s = jnp.where(mask, s, NEG_BIG)
                mx = jnp.max(s, axis=1, keepdims=True)
                pr = jnp.exp(s - mx)
                den = jnp.sum(pr, axis=1, keepdims=True)
                outs.append(_dot(pr.astype(BF16), vv) / den)
                lse_tile = jnp.where(lane == 2 * p + e, mx + jnp.log(den), lse_tile)
            o_ref[rows, cols] = jnp.where(low_half, outs[0], outs[1]).astype(BF16)
        lse_ref[rows, :] = lse_tile


def _attn_pattern(aq, ak, av, B, S, dilation):
    d = dilation
    L = S // d
    TQ = min(512, L)
    nq = L // TQ
    r = TQ // A_BLOCK
    qv = aq.reshape(B, L, d * A_WIDTH)
    kv = ak.reshape(B, L, d * A_WIDTH)
    vv = av.reshape(B, L, d * A_WIDTH)
    cur = lambda b, rr, i: (b, i, rr)
    prev = lambda b, rr, i: (b, jnp.maximum(i * r - 1, 0), rr)
    o, lse = pl.pallas_call(
        _attn_kernel,
        grid=(B, d, nq),
        in_specs=[
            pl.BlockSpec((None, TQ, A_WIDTH), cur),
            pl.BlockSpec((None, A_BLOCK, A_WIDTH), prev),
            pl.BlockSpec((None, TQ, A_WIDTH), cur),
            pl.BlockSpec((None, A_BLOCK, A_WIDTH), prev),
            pl.BlockSpec((None, TQ, A_WIDTH), cur),
        ],
        out_specs=[
            pl.BlockSpec((None, TQ, A_WIDTH), cur),
            pl.BlockSpec((None, TQ, LANES), cur),
        ],
        out_shape=[
            jax.ShapeDtypeStruct((B, L, d * A_WIDTH), BF16),
            jax.ShapeDtypeStruct((B, L, d * LANES), F32),
        ],
        scratch_shapes=[
            pltpu.VMEM((A_BLOCK + TQ, A_WIDTH), BF16),
            pltpu.VMEM((A_BLOCK + TQ, A_WIDTH), BF16),
        ],
        compiler_params=pltpu.CompilerParams(dimension_semantics=("arbitrary", "arbitrary", "arbitrary"),
                                             vmem_limit_bytes=VMEM_LIMIT),
        name=f"attn_d{d}",
    )(qv, kv, kv, vv, vv)
    return o.reshape(B * S, A_WIDTH), lse.reshape(B * S, LANES)


def _outproj_kernel(x_ref, hm_ref, o1_ref, o2_ref, o3_ref, l1_ref, l2_ref, l3_ref, az_ref, w_ref, ex_ref, out_ref):
    l1, l2, l3 = l1_ref[...], l2_ref[...], l3_ref[...]
    mx = jnp.maximum(jnp.maximum(l1, l2), l3)
    e1, e2, e3 = jnp.exp(l1 - mx), jnp.exp(l2 - mx), jnp.exp(l3 - mx)
    tot = e1 + e2 + e3
    ex = ex_ref[...]
    ha = (_dot_exact_rhs(e1 / tot, ex) * o1_ref[...].astype(F32)
          + _dot_exact_rhs(e2 / tot, ex) * o2_ref[...].astype(F32)
          + _dot_exact_rhs(e3 / tot, ex) * o3_ref[...].astype(F32))
    zz = az_ref[...].astype(F32)
    ha = ha * (zz * _sigmoid(zz))
    y = _dot(hm_ref[...], w_ref[0:M_WIDTH, :]) + _dot(ha.astype(BF16), w_ref[M_WIDTH:D_MIX, :])
    out_ref[...] = x_ref[...] + y


def _outproj(x2d, hm, outs, lses, az, w_out, expand):
    T = x2d.shape[0]
    tm = TM_PROJ
    row = lambda i: (i, 0)
    const = lambda i: (0, 0)
    return pl.pallas_call(
        _outproj_kernel,
        grid=(T // tm,),
        in_specs=[
            pl.BlockSpec((tm, D_MODEL), row),
            pl.BlockSpec((tm, M_WIDTH), row),
            pl.BlockSpec((tm, A_WIDTH), row),
            pl.BlockSpec((tm, A_WIDTH), row),
            pl.BlockSpec((tm, A_WIDTH), row),
            pl.BlockSpec((tm, LANES), row),
            pl.BlockSpec((tm, LANES), row),
            pl.BlockSpec((tm, LANES), row),
            pl.BlockSpec((tm, A_WIDTH), row),
            pl.BlockSpec((D_MIX, D_MODEL), const),
            pl.BlockSpec((LANES, A_WIDTH), const),
        ],
        out_specs=pl.BlockSpec((tm, D_MODEL), row),
        out_shape=jax.ShapeDtypeStruct((T, D_MODEL), F32),
        compiler_params=pltpu.CompilerParams(dimension_semantics=("arbitrary",), vmem_limit_bytes=VMEM_LIMIT),
        name="outproj",
    )(x2d, hm, *outs, *lses, az, w_out, expand)


def _constants():
    idx = jnp.arange(A_WIDTH)
    bd = (idx[:, None] // A_HEAD_DIM == idx[None, :] // A_HEAD_DIM).astype(BF16)
    t = jnp.arange(M_CHUNK)
    tri = (t[None, :] <= t[:, None]).astype(BF16)
    expand = (jnp.arange(LANES)[:, None] == idx[None, :] // A_HEAD_DIM).astype(BF16)
    return bd, tri, expand


def kernel(x, norm_g, w_in, gate_b, conv_w, conv_b, m_norm_g, q_norm_g, k_norm_g, w_out):
    B, S, D = x.shape
    depth = norm_g.shape[0]
    assert D == D_MODEL and S % (16 * A_BLOCK) == 0 and S % M_CHUNK == 0 and (B * S) % TM_PROJ == 0
    bd, tri, expand = _constants()
    n_gate0 = 2 * M_QK_WIDTH + 3 * M_WIDTH
    x2d = x.reshape(B * S, D)
    for l in range(depth):
        w = w_in[l]
        w_all = jnp.concatenate(
            [w[:, :n_gate0], w[:, n_gate0 + 2 * M_HEADS:],
             jnp.pad(w[:, n_gate0:n_gate0 + 2 * M_HEADS], ((0, 0), (0, GATE_PAD - 2 * M_HEADS)))],
            axis=1).astype(BF16)
        gq = (jnp.tile(q_norm_g[l], A_HEADS) * (A_HEAD_DIM ** -0.5))[None, :]
        gk = jnp.tile(k_norm_g[l], A_HEADS)[None, :]
        qk, mv, mo, mz, aq, ak, av, az, gates = _inproj(x2d, norm_g[l][None, :], w_all, bd, gq, gk)
        gb = jnp.pad(gate_b[l], (0, GATE_PAD - 2 * M_HEADS))[None, :]
        hm = _mlstm(qk, mv, mo, mz, gates, gb, conv_w[l], conv_b[l][None, :], m_norm_g[l][None, :], tri, B, S)
        outs, lses = [], []
        for _, dilation in DILATED_PATTERNS:
            o, lse = _attn_pattern(aq, ak, av, B, S, dilation)
            outs.append(o)
            lses.append(lse)
        x2d = _outproj(x2d, hm, outs, lses, az, w_out[l].astype(BF16), expand)
    return x2d.reshape(B, S, D)
```

```python
import functools

import jax
import jax.numpy as jnp
from jax import lax
from jax.experimental import pallas as pl
from jax.experimental.pallas import tpu as pltpu

F32 = jnp.float32
BF16 = jnp.bfloat16

EPS = 1e-6
D_MODEL = 1024
M_HEADS = 4
M_V_DIM = 256
M_QK_DIM = 128
M_QK_WIDTH = M_HEADS * M_QK_DIM
M_WIDTH = M_HEADS * M_V_DIM
CONV_K = 4
A_HEAD_DIM = 64
A_HEADS = 8
A_WIDTH = A_HEADS * A_HEAD_DIM
A_BLOCK = 128
DILATED_PATTERNS = ((128, 1), (512, 4), (2048, 16))
D_MIX = M_WIDTH + A_WIDTH

LANES = 128
MXU_TILE = 256
GATE_PAD = LANES
GATE_ROWS = 16
N_MAIN = 2 * M_QK_WIDTH + 3 * M_WIDTH + 4 * A_WIDTH
NEG_BIG = -1e30

V7X_VMEM_BYTES = 64 * 1024 * 1024
VMEM_LIMIT = 56 * 1024 * 1024

TM_PROJ = 512
M_CHUNK = 256


def _sigmoid(x):
    return 1.0 / (1.0 + jnp.exp(-x))


def _split3(x):
    hi = x.astype(BF16)
    r1 = x - hi.astype(F32)
    mid = r1.astype(BF16)
    lo = (r1 - mid.astype(F32)).astype(BF16)
    return hi, mid, lo


def _split2(x):
    hi = x.astype(BF16)
    return hi, (x - hi.astype(F32)).astype(BF16)


def _dot(a, b):
    return jnp.dot(a, b, preferred_element_type=F32)


def _dot_exact_rhs(x, m):
    hi, mid, lo = _split3(x)
    return _dot(hi, m) + _dot(mid, m) + _dot(lo, m)


def _inproj_kernel(tiles_per_seq, x_ref, g_ref, w_ref, bd_ref, gq_ref, gk_ref, cw_ref, cb_ref,
                   qk_ref, v_ref, o_ref, z_ref, az_ref, gate_ref,
                   aq_ref, ak_ref, av_ref, aq4_ref, ak4_ref, av4_ref, aq16_ref, ak16_ref, av16_ref, rel, cbuf):
    tm = x_ref.shape[0]
    x = x_ref[...]
    ms = jnp.mean(x * x, axis=-1, keepdims=True)
    h = (x * lax.rsqrt(ms + EPS) * g_ref[...]).astype(BF16)

    def proj(c0, c1):
        return _dot(h, w_ref[:, c0:c1])

    def head_norm(a, gain_ref):
        hi, lo = _split2(a * a)
        bd = bd_ref[...]
        half = bd.shape[0]
        ss = jnp.concatenate(
            [_dot(hi[:, s:s + half], bd) + _dot(lo[:, s:s + half], bd) for s in range(0, A_WIDTH, half)], axis=1)
        return a * lax.rsqrt(ss * (1.0 / A_HEAD_DIM) + EPS) * gain_ref[...]

    first = (pl.program_id(0) % tiles_per_seq) == 0

    @pl.when(first)
    def _():
        cbuf[0:8, :] = jnp.zeros((8, cbuf.shape[1]), F32)

    @pl.when(jnp.logical_not(first))
    def _():
        cbuf[0:8, :] = cbuf[tm:tm + 8, :]

    c = 0
    cbuf[8:8 + tm, :] = proj(c, c + 2 * M_QK_WIDTH); c += 2 * M_QK_WIDTH
    conv = cb_ref[...] + cw_ref[0:1, :] * cbuf[5:5 + tm, :]
    for j in range(1, CONV_K):
        conv = conv + cw_ref[j:j + 1, :] * cbuf[5 + j:5 + j + tm, :]
    act = conv * _sigmoid(conv)
    qk_ref[:, 0:M_QK_WIDTH] = (act[:, 0:M_QK_WIDTH] * (M_QK_DIM ** -0.5)).astype(BF16)
    qk_ref[:, M_QK_WIDTH:] = act[:, M_QK_WIDTH:].astype(BF16)

    v_ref[...] = proj(c, c + M_WIDTH).astype(BF16); c += M_WIDTH
    o_ref[...] = proj(c, c + M_WIDTH).astype(BF16); c += M_WIDTH
    z_ref[...] = proj(c, c + M_WIDTH).astype(BF16); c += M_WIDTH

    def emit_views(val, nat_ref, view_refs):
        nat_ref[...] = val.astype(BF16)
        for sl in range(A_WIDTH // LANES):
            rel[sl] = val[:, sl * LANES:(sl + 1) * LANES]
        for d, ref in view_refs:
            for r in range(d):
                for sl in range(A_WIDTH // LANES):
                    c0 = r * A_WIDTH + sl * LANES
                    ref[:, c0:c0 + LANES] = rel[sl, pl.ds(r, tm // d, stride=d), :].astype(BF16)

    emit_views(head_norm(proj(c, c + A_WIDTH), gq_ref), aq_ref, ((4, aq4_ref), (16, aq16_ref))); c += A_WIDTH
    emit_views(head_norm(proj(c, c + A_WIDTH), gk_ref), ak_ref, ((4, ak4_ref), (16, ak16_ref))); c += A_WIDTH
    emit_views(proj(c, c + A_WIDTH), av_ref, ((4, av4_ref), (16, av16_ref))); c += A_WIDTH
    az_ref[...] = proj(c, c + A_WIDTH).astype(BF16); c += A_WIDTH
    gate_ref[...] = proj(c, c + GATE_PAD)


def _inproj(x2d, norm_g, w_all, bd, gq, gk, conv_w, conv_b, S):
    T = x2d.shape[0]
    tm = TM_PROJ
    const = lambda i: (0, 0)
    row = lambda i: (i, 0)
    widths = (2 * M_QK_WIDTH, M_WIDTH, M_WIDTH, M_WIDTH, A_WIDTH)
    out_shape = [jax.ShapeDtypeStruct((T, w), BF16) for w in widths] + [jax.ShapeDtypeStruct((T, GATE_PAD), F32)]
    out_specs = [pl.BlockSpec((tm, w), row) for w in widths] + [pl.BlockSpec((tm, GATE_PAD), row)]
    for _, d in DILATED_PATTERNS:
        out_shape += [jax.ShapeDtypeStruct((T // d, d * A_WIDTH), BF16)] * 3
        out_specs += [pl.BlockSpec((tm // d, d * A_WIDTH), row)] * 3
    return pl.pallas_call(
        functools.partial(_inproj_kernel, S // tm),
        grid=(T // tm,),
        in_specs=[
            pl.BlockSpec((tm, D_MODEL), row),
            pl.BlockSpec((1, D_MODEL), const),
            pl.BlockSpec((D_MODEL, N_MAIN + GATE_PAD), const, pipeline_mode=pl.Buffered(1)),
            pl.BlockSpec(bd.shape, const),
            pl.BlockSpec((1, A_WIDTH), const),
            pl.BlockSpec((1, A_WIDTH), const),
            pl.BlockSpec((CONV_K, 2 * M_QK_WIDTH), const),
            pl.BlockSpec((1, 2 * M_QK_WIDTH), const),
        ],
        out_specs=out_specs,
        out_shape=out_shape,
        scratch_shapes=[
            pltpu.VMEM((A_WIDTH // LANES, tm, LANES), F32),
            pltpu.VMEM((tm + 8, 2 * M_QK_WIDTH), F32),
        ],
        compiler_params=pltpu.CompilerParams(dimension_semantics=("arbitrary",), vmem_limit_bytes=VMEM_LIMIT),
        name="inproj",
    )(x2d, norm_g, w_all, bd, gq, gk, conv_w, conv_b)


def _mlstm_kernel(qk_ref, v_ref, o_ref, z_ref, gate_ref, gb_ref, mg_ref, triu_ref,
                  out_ref, c_ref, n_ref, m_ref):
    L = qk_ref.shape[0]

    @pl.when(pl.program_id(1) == 0)
    def _():
        c_ref[...] = jnp.zeros(c_ref.shape, F32)
        n_ref[...] = jnp.zeros(n_ref.shape, F32)
        m_ref[...] = jnp.zeros(m_ref.shape, F32)

    gt = (gate_ref[...] + gb_ref[...]).T[0:GATE_ROWS, :]
    logf = jnp.minimum(gt, 0.0) - jnp.log1p(jnp.exp(-jnp.abs(gt)))
    bcum = _dot_exact_rhs(logf, triu_ref[...])
    gbt = jnp.where(lax.broadcasted_iota(jnp.int32, (GATE_ROWS, L), 0) < M_HEADS, gt, bcum)
    gb = jnp.concatenate([gbt, jnp.zeros((LANES - GATE_ROWS, L), F32)], axis=0).T
    lane = lax.broadcasted_iota(jnp.int32, (L, LANES), 1)

    def col(idx):
        return jnp.sum(jnp.where(lane == idx, gb, 0.0), axis=1, keepdims=True)

    t_idx = lax.broadcasted_iota(jnp.int32, (L, L), 0)
    s_idx = lax.broadcasted_iota(jnp.int32, (L, L), 1)
    causal = s_idx <= t_idx
    last_row = lax.broadcasted_iota(jnp.int32, (L, 1), 0) == (L - 1)

    for h in range(M_HEADS):
        i_col, b_col = col(h), col(M_HEADS + h)
        i_row, b_row = gbt[h:h + 1, :], gbt[M_HEADS + h:M_HEADS + h + 1, :]
        m_prev = m_ref[h]
        c_prev = c_ref[h]
        n_prev = n_ref[h]

        q = qk_ref[:, h * M_QK_DIM:(h + 1) * M_QK_DIM]
        k = qk_ref[:, M_QK_WIDTH + h * M_QK_DIM:M_QK_WIDTH + (h + 1) * M_QK_DIM]
        k_f = k.astype(F32)
        v = v_ref[:, h * M_V_DIM:(h + 1) * M_V_DIM]

        log_intra = jnp.where(causal, b_col - b_row + i_row, NEG_BIG)
        log_inter = b_col + m_prev
        m_t = jnp.maximum(log_inter, jnp.max(log_intra, axis=1, keepdims=True))
        w_intra = jnp.exp(log_intra - m_t)
        w_inter = jnp.exp(log_inter - m_t)

        s = lax.dot_general(q, k, (((1,), (1,)), ((), ())), preferred_element_type=F32) * w_intra
        num = _dot(s.astype(BF16), v) + w_inter * _dot(q, c_prev.astype(BF16))
        den = jnp.sum(s, axis=1, keepdims=True) + w_inter * jnp.sum(q.astype(F32) * n_prev, axis=1, keepdims=True)
        hid = num / jnp.maximum(jnp.abs(den), jnp.exp(-m_t))

        b_last = jnp.sum(jnp.where(last_row, b_col, 0.0), axis=0, keepdims=True)
        log_state = b_last - b_col + i_col
        m_new = jnp.maximum(b_last + m_prev, jnp.max(log_state, axis=0, keepdims=True))
        w_s = jnp.exp(log_state - m_new)
        decay = jnp.exp(b_last + m_prev - m_new)
        kw = k_f * w_s
        c_ref[h] = decay * c_prev + lax.dot_general(kw.astype(BF16), v, (((0,), (0,)), ((), ())),
                                                    preferred_element_type=F32)
        n_ref[h] = decay * n_prev + jnp.sum(kw, axis=0, keepdims=True)
        m_ref[h] = m_new

        sl = slice(h * M_V_DIM, (h + 1) * M_V_DIM)
        hg = _sigmoid(o_ref[:, sl].astype(F32)) * hid
        hn = hg * lax.rsqrt(jnp.mean(hg * hg, axis=-1, keepdims=True) + EPS) * mg_ref[:, sl]
        zz = z_ref[:, sl].astype(F32)
        out_ref[:, sl] = (hn * (zz * _sigmoid(zz))).astype(BF16)


def _mlstm(qk, v, o, z, gates, gate_b, m_norm_g, triu, B, S):
    L = M_CHUNK
    nc = S // L
    row = lambda b, s: (b * nc + s, 0)
    const = lambda b, s: (0, 0)
    return pl.pallas_call(
        _mlstm_kernel,
        grid=(B, nc),
        in_specs=[
            pl.BlockSpec((L, 2 * M_QK_WIDTH), row),
            pl.BlockSpec((L, M_WIDTH), row),
            pl.BlockSpec((L, M_WIDTH), row),
            pl.BlockSpec((L, M_WIDTH), row),
            pl.BlockSpec((L, GATE_PAD), row),
            pl.BlockSpec((1, GATE_PAD), const),
            pl.BlockSpec((1, M_WIDTH), const),
            pl.BlockSpec((L, L), const),
        ],
        out_specs=pl.BlockSpec((L, M_WIDTH), row),
        out_shape=jax.ShapeDtypeStruct((B * S, M_WIDTH), BF16),
        scratch_shapes=[
            pltpu.VMEM((M_HEADS, M_QK_DIM, M_V_DIM), F32),
            pltpu.VMEM((M_HEADS, 1, M_QK_DIM), F32),
            pltpu.VMEM((M_HEADS, 1, 1), F32),
        ],
        compiler_params=pltpu.CompilerParams(dimension_semantics=("arbitrary", "arbitrary"),
                                             vmem_limit_bytes=VMEM_LIMIT),
        name="mlstm",
    )(qk, v, o, z, gates, gate_b, m_norm_g, triu)


def _attn_kernel(q_ref, kp_ref, kc_ref, vp_ref, vc_ref, o_ref, lse_ref, kbuf, vbuf):
    TQ = q_ref.shape[0]
    i = pl.program_id(2)
    kbuf[0:A_BLOCK, :] = kp_ref[...]
    kbuf[A_BLOCK:A_BLOCK + TQ, :] = kc_ref[...]
    vbuf[0:A_BLOCK, :] = vp_ref[...]
    vbuf[A_BLOCK:A_BLOCK + TQ, :] = vc_ref[...]

    qi = lax.broadcasted_iota(jnp.int32, (A_BLOCK, 2 * A_BLOCK), 0)
    kc = lax.broadcasted_iota(jnp.int32, (A_BLOCK, 2 * A_BLOCK), 1)
    band = (kc >= qi) & (kc <= qi + A_BLOCK)
    first_lim = jnp.where(i > 0, 0, A_BLOCK)
    band_first = band & (kc >= first_lim)
    lane = lax.broadcasted_iota(jnp.int32, (A_BLOCK, LANES), 1)
    low_half = lane < A_HEAD_DIM

    for j in range(TQ // A_BLOCK):
        mask = band_first if j == 0 else band
        rows = slice(j * A_BLOCK, (j + 1) * A_BLOCK)
        krows = slice(j * A_BLOCK, (j + 2) * A_BLOCK)
        lse_tile = jnp.zeros((A_BLOCK, LANES), F32)
        for p in range(A_HEADS // 2):
            cols = slice(p * LANES, (p + 1) * LANES)
            q2 = q_ref[rows, cols]
            kk = kbuf[krows, cols]
            vv = vbuf[krows, cols]
            outs = []
            for e in range(2):
                sel = low_half if e == 0 else jnp.logical_not(low_half)
                qm = jnp.where(sel, q2, jnp.zeros_like(q2))
                s = lax.dot_general(qm, kk, (((1,), (1,)), ((), ())), preferred_element_type=F32)
                s = jnp.where(mask, s, NEG_BIG)
                mx = jnp.max(s, axis=1, keepdims=True)
                pr = jnp.exp(s - mx)
                den = jnp.sum(pr, axis=1, keepdims=True)
                outs.append(_dot(pr.astype(BF16), vv) / den)
                lse_tile = jnp.where(lane == 2 * p + e, mx + jnp.log(den), lse_tile)
            o_ref[rows, cols] = jnp.where(low_half, outs[0], outs[1]).astype(BF16)
        lse_ref[rows, :] = lse_tile


def _attn_pattern(qv, kv, vv, B, S, dilation):
    d = dilation
    L = S // d
    TQ = min(512, L)
    nq = L // TQ
    r = TQ // A_BLOCK
    cur = lambda b, rr, i: (b * nq + i, rr)
    prev = lambda b, rr, i: (b * (L // A_BLOCK) + jnp.maximum(i * r - 1, 0), rr)
    return pl.pallas_call(
        _attn_kernel,
        grid=(B, d, nq),
        in_specs=[
            pl.BlockSpec((TQ, A_WIDTH), cur),
            pl.BlockSpec((A_BLOCK, A_WIDTH), prev),
            pl.BlockSpec((TQ, A_WIDTH), cur),
            pl.BlockSpec((A_BLOCK, A_WIDTH), prev),
            pl.BlockSpec((TQ, A_WIDTH), cur),
        ],
        out_specs=[
            pl.BlockSpec((TQ, A_WIDTH), cur),
            pl.BlockSpec((TQ, LANES), cur),
        ],
        out_shape=[
            jax.ShapeDtypeStruct((B * L, d * A_WIDTH), BF16),
            jax.ShapeDtypeStruct((B * L, d * LANES), F32),
        ],
        scratch_shapes=[
            pltpu.VMEM((A_BLOCK + TQ, A_WIDTH), BF16),
            pltpu.VMEM((A_BLOCK + TQ, A_WIDTH), BF16),
        ],
        compiler_params=pltpu.CompilerParams(dimension_semantics=("arbitrary", "arbitrary", "arbitrary"),
                                             vmem_limit_bytes=VMEM_LIMIT),
        name=f"attn_d{d}",
    )(qv, kv, kv, vv, vv)


def _outproj_kernel(x_ref, hm_ref, o1_ref, o4_ref, o16_ref, l1_ref, l4_ref, l16_ref, az_ref, w_ref, ex_ref,
                    out_ref, o4n, o16n, l4n, l16n, ha_scr):
    tm = x_ref.shape[0]
    for d, o_ref, l_ref, o_nat, l_nat in ((4, o4_ref, l4_ref, o4n, l4n), (16, o16_ref, l16_ref, o16n, l16n)):
        for r in range(d):
            l_nat[pl.ds(r, tm // d, stride=d), :] = l_ref[:, r * LANES:(r + 1) * LANES]
            for sl in range(A_WIDTH // LANES):
                c0 = r * A_WIDTH + sl * LANES
                o_nat[sl, pl.ds(r, tm // d, stride=d), :] = o_ref[:, c0:c0 + LANES].astype(F32)

    l1, l2, l3 = l1_ref[...], l4n[...], l16n[...]
    mx = jnp.maximum(jnp.maximum(l1, l2), l3)
    e1, e2, e3 = jnp.exp(l1 - mx), jnp.exp(l2 - mx), jnp.exp(l3 - mx)
    tot = e1 + e2 + e3
    ex = ex_ref[...]

    def expand(w):
        hi, lo = _split2(w)
        return _dot(hi, ex) + _dot(lo, ex)

    wts = (expand(e1 / tot), expand(e2 / tot), expand(e3 / tot))
    for sl in range(A_WIDTH // LANES):
        cols = slice(sl * LANES, (sl + 1) * LANES)
        acc = (wts[0][:, cols] * o1_ref[:, cols].astype(F32) + wts[1][:, cols] * o4n[sl]
               + wts[2][:, cols] * o16n[sl])
        zz = az_ref[:, cols].astype(F32)
        ha_scr[:, cols] = (acc * (zz * _sigmoid(zz))).astype(BF16)

    y = _dot(hm_ref[...], w_ref[0:M_WIDTH, :]) + _dot(ha_scr[...], w_ref[M_WIDTH:D_MIX, :])
    out_ref[...] = x_ref[...] + y


def _outproj(x2d, hm, outs, lses, az, w_out, expand):
    T = x2d.shape[0]
    tm = TM_PROJ
    row = lambda i: (i, 0)
    const = lambda i: (0, 0)
    dils = [d for _, d in DILATED_PATTERNS]
    return pl.pallas_call(
        _outproj_kernel,
        grid=(T // tm,),
        in_specs=[pl.BlockSpec((tm, D_MODEL), row), pl.BlockSpec((tm, M_WIDTH), row)]
        + [pl.BlockSpec((tm // d, d * A_WIDTH), row) for d in dils]
        + [pl.BlockSpec((tm // d, d * LANES), row) for d in dils]
        + [pl.BlockSpec((tm, A_WIDTH), row), pl.BlockSpec((D_MIX, D_MODEL), const),
           pl.BlockSpec((LANES, A_WIDTH), const)],
        out_specs=pl.BlockSpec((tm, D_MODEL), row),
        out_shape=jax.ShapeDtypeStruct((T, D_MODEL), F32),
        scratch_shapes=[
            pltpu.VMEM((A_WIDTH // LANES, tm, LANES), F32),
            pltpu.VMEM((A_WIDTH // LANES, tm, LANES), F32),
            pltpu.VMEM((tm, LANES), F32),
            pltpu.VMEM((tm, LANES), F32),
            pltpu.VMEM((tm, A_WIDTH), BF16),
        ],
        compiler_params=pltpu.CompilerParams(dimension_semantics=("arbitrary",), vmem_limit_bytes=VMEM_LIMIT),
        name="outproj",
    )(x2d, hm, *outs, *lses, az, w_out, expand)


def _constants():
    idx = jnp.arange(MXU_TILE)
    bd = (idx[:, None] // A_HEAD_DIM == idx[None, :] // A_HEAD_DIM).astype(BF16)
    t = jnp.arange(M_CHUNK)
    triu = (t[:, None] <= t[None, :]).astype(BF16)
    expand = (jnp.arange(LANES)[:, None] == jnp.arange(A_WIDTH)[None, :] // A_HEAD_DIM).astype(BF16)
    return bd, triu, expand


def kernel(x, norm_g, w_in, gate_b, conv_w, conv_b, m_norm_g, q_norm_g, k_norm_g, w_out):
    B, S, D = x.shape
    depth = norm_g.shape[0]
    assert D == D_MODEL and S % M_CHUNK == 0 and S % TM_PROJ == 0
    for window, dilation in DILATED_PATTERNS:
        assert window == dilation * A_BLOCK and S % (dilation * A_BLOCK) == 0 and TM_PROJ % (16 * dilation) == 0
    bd, triu, expand = _constants()
    n_gate0 = 2 * M_QK_WIDTH + 3 * M_WIDTH
    x2d = x.reshape(B * S, D)
    for l in range(depth):
        w = w_in[l]
        w_all = jnp.concatenate(
            [w[:, :n_gate0], w[:, n_gate0 + 2 * M_HEADS:],
             jnp.pad(w[:, n_gate0:n_gate0 + 2 * M_HEADS], ((0, 0), (0, GATE_PAD - 2 * M_HEADS)))],
            axis=1).astype(BF16)
        gq = (jnp.tile(q_norm_g[l], A_HEADS) * (A_HEAD_DIM ** -0.5))[None, :]
        gk = jnp.tile(k_norm_g[l], A_HEADS)[None, :]
        qk, mv, mo, mz, az, gates, *qkv_views = _inproj(x2d, norm_g[l][None, :], w_all, bd, gq, gk,
                                                        conv_w[l], conv_b[l][None, :], S)
        gb = jnp.pad(gate_b[l], (0, GATE_PAD - 2 * M_HEADS))[None, :]
        hm = _mlstm(qk, mv, mo, mz, gates, gb, m_norm_g[l][None, :], triu, B, S)
        outs, lses = [], []
        for n, (_, dilation) in enumerate(DILATED_PATTERNS):
            o, lse = _attn_pattern(*qkv_views[3 * n:3 * n + 3], B, S, dilation)
            outs.append(o)
            lses.append(lse)
        x2d = _outproj(x2d, hm, outs, lses, az, w_out[l].astype(BF16), expand)
    return x2d.reshape(B, S, D)
```

```python
import functools

import jax
import jax.numpy as jnp
from jax import lax
from jax.experimental import pallas as pl
from jax.experimental.pallas import tpu as pltpu

F32 = jnp.float32
BF16 = jnp.bfloat16

EPS = 1e-6
D_MODEL = 1024
M_HEADS = 4
M_V_DIM = 256
M_QK_DIM = 128
M_QK_WIDTH = M_HEADS * M_QK_DIM
M_WIDTH = M_HEADS * M_V_DIM
CONV_K = 4
A_HEAD_DIM = 64
A_HEADS = 8
A_WIDTH = A_HEADS * A_HEAD_DIM
A_BLOCK = 128
DILATED_PATTERNS = ((128, 1), (512, 4), (2048, 16))
D_MIX = M_WIDTH + A_WIDTH

LANES = 128
MXU_TILE = 256
GATE_PAD = LANES
N_MAIN = 2 * M_QK_WIDTH + 3 * M_WIDTH + 4 * A_WIDTH
NEG_BIG = -1e30
LOG2E = 1.4426950408889634
LN2 = 0.6931471805599453

V7X_VMEM_BYTES = 64 * 1024 * 1024
VMEM_LIMIT = V7X_VMEM_BYTES * 7 // 8

TM_PROJ = 512
M_CHUNK = 256
M_SEQS_PER_STEP = 1


def _sigmoid(x):
    return 1.0 / (1.0 + jnp.exp(-x))


def _split3(x):
    hi = x.astype(BF16)
    r1 = x - hi.astype(F32)
    mid = r1.astype(BF16)
    lo = (r1 - mid.astype(F32)).astype(BF16)
    return hi, mid, lo


def _split2(x):
    hi = x.astype(BF16)
    return hi, (x - hi.astype(F32)).astype(BF16)


def _dot(a, b):
    return jnp.dot(a, b, preferred_element_type=F32)


def _inproj_kernel(tiles_per_seq, x_ref, g_ref, w_ref, bd_ref, gq_ref, gk_ref, cw_ref, cb_ref, gb_ref, tril_ref,
                   qk_ref, v_ref, o_ref, z_ref, az_ref, gate_ref,
                   aq_ref, ak_ref, av_ref, aq4_ref, ak4_ref, av4_ref, aq16_ref, ak16_ref, av16_ref, rel, cbuf, h_ref):
    tm = x_ref.shape[0]
    x = x_ref[...]
    h_ref[...] = (x * g_ref[...]).astype(BF16)
    rs = lax.rsqrt(jnp.mean(x * x, axis=-1, keepdims=True) + EPS)

    def proj(c0, c1):
        return _dot(h_ref[...], w_ref[:, c0:c1]) * rs

    def head_norm(a, gain_ref):
        hi, lo = _split2(a * a)
        bd = bd_ref[...]
        half = bd.shape[0]
        ss = jnp.concatenate(
            [_dot(hi[:, s:s + half], bd) + _dot(lo[:, s:s + half], bd) for s in range(0, A_WIDTH, half)], axis=1)
        return a * lax.rsqrt(ss * (1.0 / A_HEAD_DIM) + EPS) * gain_ref[...]

    def emit_views(val, nat_ref, view_refs, rel):
        nat_ref[...] = val.astype(BF16)
        for sl in range(A_WIDTH // LANES):
            rel[sl] = val[:, sl * LANES:(sl + 1) * LANES]
        for d, ref in view_refs:
            for r in range(d):
                for sl in range(A_WIDTH // LANES):
                    c0 = r * A_WIDTH + sl * LANES
                    ref[:, c0:c0 + LANES] = rel[sl, pl.ds(r, tm // d, stride=d), :].astype(BF16)

    c_qk, c_v, c_o, c_z = 0, 2 * M_QK_WIDTH, 2 * M_QK_WIDTH + M_WIDTH, 2 * M_QK_WIDTH + 2 * M_WIDTH
    c_aq = c_z + M_WIDTH
    c_ak, c_av, c_az, c_gate = c_aq + A_WIDTH, c_aq + 2 * A_WIDTH, c_aq + 3 * A_WIDTH, c_aq + 4 * A_WIDTH

    emit_views(head_norm(proj(c_aq, c_ak), gq_ref), aq_ref, ((4, aq4_ref), (16, aq16_ref)), rel.at[0])
    emit_views(head_norm(proj(c_ak, c_av), gk_ref), ak_ref, ((4, ak4_ref), (16, ak16_ref)), rel.at[1])
    emit_views(proj(c_av, c_az), av_ref, ((4, av4_ref), (16, av16_ref)), rel.at[2])

    first = (pl.program_id(0) % tiles_per_seq) == 0

    @pl.when(first)
    def _():
        cbuf[0:8, :] = jnp.zeros((8, cbuf.shape[1]), F32)

    @pl.when(jnp.logical_not(first))
    def _():
        cbuf[0:8, :] = cbuf[tm:tm + 8, :]

    cbuf[8:8 + tm, :] = proj(c_qk, c_v)
    conv = cb_ref[...] + cw_ref[0:1, :] * cbuf[5:5 + tm, :]
    for j in range(1, CONV_K):
        conv = conv + cw_ref[j:j + 1, :] * cbuf[5 + j:5 + j + tm, :]
    act = conv * _sigmoid(conv)
    qk_ref[:, 0:M_QK_WIDTH] = (act[:, 0:M_QK_WIDTH] * (M_QK_DIM ** -0.5)).astype(BF16)
    qk_ref[:, M_QK_WIDTH:] = act[:, M_QK_WIDTH:].astype(BF16)

    g = proj(c_gate, c_gate + GATE_PAD) + gb_ref[...]
    logf = (jnp.minimum(g, 0.0) - jnp.log1p(jnp.exp(-jnp.abs(g)))) * LOG2E
    hi, mid, lo = _split3(logf)
    tril = tril_ref[...]
    L = tril.shape[0]
    lane = lax.broadcasted_iota(jnp.int32, (L, LANES), 1)
    for c0 in range(0, tm, L):
        cum = _dot(tril, hi[c0:c0 + L]) + _dot(tril, mid[c0:c0 + L]) + _dot(tril, lo[c0:c0 + L])
        gate_ref[c0:c0 + L, :] = jnp.where(lane < M_HEADS, g[c0:c0 + L] * LOG2E, cum)

    az_ref[...] = proj(c_az, c_gate).astype(BF16)
    v_ref[...] = proj(c_v, c_o).astype(BF16)
    o_ref[...] = proj(c_o, c_z).astype(BF16)
    z_ref[...] = proj(c_z, c_aq).astype(BF16)


def _inproj(x2d, norm_g, w_all, bd, gq, gk, conv_w, conv_b, gate_b, tril, S):
    T = x2d.shape[0]
    tm = TM_PROJ
    assert tm % tril.shape[0] == 0
    const = lambda i: (0, 0)
    row = lambda i: (i, 0)
    widths = (2 * M_QK_WIDTH, M_WIDTH, M_WIDTH, M_WIDTH, A_WIDTH)
    out_shape = [jax.ShapeDtypeStruct((T, w), BF16) for w in widths] + [jax.ShapeDtypeStruct((T, GATE_PAD), F32)]
    out_specs = [pl.BlockSpec((tm, w), row) for w in widths] + [pl.BlockSpec((tm, GATE_PAD), row)]
    for _, d in DILATED_PATTERNS:
        out_shape += [jax.ShapeDtypeStruct((T // d, d * A_WIDTH), BF16)] * 3
        out_specs += [pl.BlockSpec((tm // d, d * A_WIDTH), row)] * 3
    return pl.pallas_call(
        functools.partial(_inproj_kernel, S // tm),
        grid=(T // tm,),
        in_specs=[
            pl.BlockSpec((tm, D_MODEL), row),
            pl.BlockSpec((1, D_MODEL), const),
            pl.BlockSpec((D_MODEL, N_MAIN + GATE_PAD), const, pipeline_mode=pl.Buffered(1)),
            pl.BlockSpec(bd.shape, const),
            pl.BlockSpec((1, A_WIDTH), const),
            pl.BlockSpec((1, A_WIDTH), const),
            pl.BlockSpec((CONV_K, 2 * M_QK_WIDTH), const),
            pl.BlockSpec((1, 2 * M_QK_WIDTH), const),
            pl.BlockSpec((1, GATE_PAD), const),
            pl.BlockSpec(tril.shape, const),
        ],
        out_specs=out_specs,
        out_shape=out_shape,
        scratch_shapes=[
            pltpu.VMEM((3, A_WIDTH // LANES, tm, LANES), F32),
            pltpu.VMEM((tm + 8, 2 * M_QK_WIDTH), F32),
            pltpu.VMEM((tm, D_MODEL), BF16),
        ],
        compiler_params=pltpu.CompilerParams(dimension_semantics=("arbitrary",), vmem_limit_bytes=VMEM_LIMIT),
        name="inproj",
    )(x2d, norm_g, w_all, bd, gq, gk, conv_w, conv_b, gate_b, tril)


def _mlstm_kernel(qk_ref, v_ref, o_ref, z_ref, gate_ref, mg_ref, out_ref, c_ref, n_ref, m_ref, cbias_ref):
    nseq, L = qk_ref.shape[0], qk_ref.shape[1]

    @pl.when(pl.program_id(1) == 0)
    def _():
        c_ref[...] = jnp.zeros(c_ref.shape, F32)
        n_ref[...] = jnp.zeros(n_ref.shape, F32)
        m_ref[...] = jnp.zeros(m_ref.shape, F32)

    t_idx = lax.broadcasted_iota(jnp.int32, (L, L), 0)
    s_idx = lax.broadcasted_iota(jnp.int32, (L, L), 1)
    cbias_ref[...] = jnp.where(s_idx <= t_idx, 0.0, NEG_BIG)
    lane = lax.broadcasted_iota(jnp.int32, (L, LANES), 1)
    ones = jnp.ones((L, LANES), BF16)

    units = [(i, h) for i in range(nseq) for h in range(M_HEADS)]
    idx = range(len(units))
    q = [qk_ref[i, :, h * M_QK_DIM:(h + 1) * M_QK_DIM] for i, h in units]
    k = [qk_ref[i, :, M_QK_WIDTH + h * M_QK_DIM:M_QK_WIDTH + (h + 1) * M_QK_DIM] for i, h in units]
    v = [v_ref[i, :, h * M_V_DIM:(h + 1) * M_V_DIM] for i, h in units]
    m_prev = [m_ref[u] for u in idx]
    c_prev = [c_ref[u] for u in idx]
    n_prev = [n_ref[u] for u in idx]

    qk = [lax.dot_general(q[u], k[u], (((1,), (1,)), ((), ())), preferred_element_type=F32) for u in idx]
    qc = [_dot(q[u], jnp.concatenate([c_prev[u], n_prev[u]], axis=1).astype(BF16)) for u in idx]

    gb = [gate_ref[i] for i in range(nseq)]
    gbt = [g.T[0:2 * M_HEADS, :] for g in gb]
    b_row = [gbt[i][M_HEADS + h:M_HEADS + h + 1, :] for i, h in units]
    a_row = [gbt[i][h:h + 1, :] - b_row[u] for u, (i, h) in enumerate(units)]

    for u in idx:
        b_last = b_row[u][:, L - 1:L]
        log_state = b_last + a_row[u]
        m_new = jnp.maximum(b_last + m_prev[u], jnp.max(log_state, axis=1, keepdims=True))
        decay = jnp.exp2(b_last + m_prev[u] - m_new)
        kw = k[u].astype(F32).T * jnp.exp2(log_state - m_new)
        c_ref[u] = decay * c_prev[u] + _dot(kw.astype(BF16), v[u])
        n_ref[u] = decay * n_prev[u] + jnp.sum(kw, axis=1, keepdims=True)
        m_ref[u] = m_new

    s, m_t, w_inter = [], [], []
    for u, (i, h) in enumerate(units):
        b_col = jnp.sum(jnp.where(lane == M_HEADS + h, gb[i], 0.0), axis=1, keepdims=True)
        log_intra = (b_col + a_row[u]) + cbias_ref[...]
        log_inter = b_col + m_prev[u]
        m_t.append(jnp.maximum(log_inter, jnp.max(log_intra, axis=1, keepdims=True)))
        w_inter.append(jnp.exp2(log_inter - m_t[u]))
        s.append((qk[u] * jnp.exp2(log_intra - m_t[u])).astype(BF16))
    sv = [_dot(s[u], jnp.concatenate([v[u], ones], axis=1)) for u in idx]

    for u, (i, h) in enumerate(units):
        num = sv[u][:, :M_V_DIM] + w_inter[u] * qc[u][:, :M_V_DIM]
        den = sv[u][:, M_V_DIM:] + w_inter[u] * qc[u][:, M_V_DIM:]
        den = jnp.maximum(jnp.abs(den), jnp.exp2(-m_t[u]))
        hid = num / jnp.concatenate([den] * (M_V_DIM // LANES), axis=1)
        sl = slice(h * M_V_DIM, (h + 1) * M_V_DIM)
        hg = _sigmoid(o_ref[i, :, sl].astype(F32)) * hid
        hn = hg * lax.rsqrt(jnp.mean(hg * hg, axis=-1, keepdims=True) + EPS) * mg_ref[:, sl]
        zz = z_ref[i, :, sl].astype(F32)
        out_ref[i, :, sl] = (hn * (zz * _sigmoid(zz))).astype(BF16)


def _mlstm(qk, v, o, z, gates, m_norm_g, B, S):
    L = M_CHUNK
    nseq = M_SEQS_PER_STEP
    const = lambda g, s: (0, 0)
    seqs = lambda g, s: (g, 0, s, 0)
    operands = [a.reshape(B // nseq, nseq, S, a.shape[-1]) for a in (qk, v, o, z, gates)]
    out = pl.pallas_call(
        _mlstm_kernel,
        grid=(B // nseq, S // L),
        in_specs=[pl.BlockSpec((None, nseq, L, a.shape[-1]), seqs) for a in operands]
        + [pl.BlockSpec((1, M_WIDTH), const)],
        out_specs=pl.BlockSpec((None, nseq, L, M_WIDTH), seqs),
        out_shape=jax.ShapeDtypeStruct((B // nseq, nseq, S, M_WIDTH), BF16),
        scratch_shapes=[
            pltpu.VMEM((nseq * M_HEADS, M_QK_DIM, M_V_DIM), F32),
            pltpu.VMEM((nseq * M_HEADS, M_QK_DIM, LANES), F32),
            pltpu.VMEM((nseq * M_HEADS, 1, 1), F32),
            pltpu.VMEM((L, L), F32),
        ],
        compiler_params=pltpu.CompilerParams(dimension_semantics=("arbitrary", "arbitrary"),
                                             vmem_limit_bytes=VMEM_LIMIT),
        name="mlstm",
    )(*operands, m_norm_g)
    return out.reshape(B * S, M_WIDTH)


def _attn_kernel(q_ref, kp_ref, kc_ref, vp_ref, vc_ref, o_ref, lse_ref, bias_ref, bias0_ref):
    TQ = q_ref.shape[0]
    i = pl.program_id(2)
    qi = lax.broadcasted_iota(jnp.int32, (2 * A_BLOCK, 2 * A_BLOCK), 0) % A_BLOCK
    kc = lax.broadcasted_iota(jnp.int32, (2 * A_BLOCK, 2 * A_BLOCK), 1)
    band = (kc >= qi) & (kc <= qi + A_BLOCK)
    bias_ref[...] = jnp.where(band, 0.0, NEG_BIG)
    bias0_ref[...] = jnp.where(band & (kc >= jnp.where(i > 0, 0, A_BLOCK)), 0.0, NEG_BIG)
    lane = lax.broadcasted_iota(jnp.int32, (A_BLOCK, LANES), 1)
    low_half = lane < A_HEAD_DIM
    zero = jnp.zeros((A_BLOCK, LANES), BF16)
    ones = jnp.ones((2 * A_BLOCK, LANES), BF16)

    for j in range(TQ // A_BLOCK):
        bias = bias0_ref if j == 0 else bias_ref
        rows = slice(j * A_BLOCK, (j + 1) * A_BLOCK)
        mx_tile = jnp.zeros((A_BLOCK, LANES), F32)
        den_tile = jnp.ones((A_BLOCK, LANES), F32)
        for p in range(A_HEADS // 2):
            cols = slice(p * LANES, (p + 1) * LANES)
            if j == 0:
                kk = jnp.concatenate([kp_ref[:, cols], kc_ref[0:A_BLOCK, cols]], axis=0)
                vv = jnp.concatenate([vp_ref[:, cols], vc_ref[0:A_BLOCK, cols]], axis=0)
            else:
                kk = kc_ref[(j - 1) * A_BLOCK:(j + 1) * A_BLOCK, cols]
                vv = vc_ref[(j - 1) * A_BLOCK:(j + 1) * A_BLOCK, cols]
            q2 = q_ref[rows, cols]
            qs = jnp.concatenate([jnp.where(low_half, q2, zero), jnp.where(low_half, zero, q2)], axis=0)
            s = lax.dot_general(qs, kk, (((1,), (1,)), ((), ())), preferred_element_type=F32)
            s = s + bias[...]
            mx = jnp.max(s, axis=1, keepdims=True)
            pr = jnp.exp2(s - mx).astype(BF16)
            pv = _dot(pr, jnp.concatenate([vv, ones], axis=1))
            den = pv[:, LANES:]
            out = pv[:, :LANES] / den
            o_ref[rows, cols] = jnp.where(low_half, out[0:A_BLOCK], out[A_BLOCK:]).astype(BF16)
            mx_tile = jnp.where(lane == 2 * p, mx[0:A_BLOCK], jnp.where(lane == 2 * p + 1, mx[A_BLOCK:], mx_tile))
            den_tile = jnp.where(lane == 2 * p, den[0:A_BLOCK], jnp.where(lane == 2 * p + 1, den[A_BLOCK:], den_tile))
        lse_ref[rows, :] = mx_tile * LN2 + jnp.log(den_tile)


def _attn_pattern(qv, kv, vv, B, S, dilation):
    d = dilation
    L = S // d
    TQ = min(512, L)
    nq = L // TQ
    r = TQ // A_BLOCK
    cur = lambda b, rr, i: (b * nq + i, rr)
    prev = lambda b, rr, i: (b * (L // A_BLOCK) + jnp.maximum(i * r - 1, 0), rr)
    return pl.pallas_call(
        _attn_kernel,
        grid=(B, d, nq),
        in_specs=[
            pl.BlockSpec((TQ, A_WIDTH), cur),
            pl.BlockSpec((A_BLOCK, A_WIDTH), prev),
            pl.BlockSpec((TQ, A_WIDTH), cur),
            pl.BlockSpec((A_BLOCK, A_WIDTH), prev),
            pl.BlockSpec((TQ, A_WIDTH), cur),
        ],
        out_specs=[
            pl.BlockSpec((TQ, A_WIDTH), cur),
            pl.BlockSpec((TQ, LANES), cur),
        ],
        out_shape=[
            jax.ShapeDtypeStruct((B * L, d * A_WIDTH), BF16),
            jax.ShapeDtypeStruct((B * L, d * LANES), F32),
        ],
        scratch_shapes=[
            pltpu.VMEM((2 * A_BLOCK, 2 * A_BLOCK), F32),
            pltpu.VMEM((2 * A_BLOCK, 2 * A_BLOCK), F32),
        ],
        compiler_params=pltpu.CompilerParams(dimension_semantics=("arbitrary", "arbitrary", "arbitrary"),
                                             vmem_limit_bytes=VMEM_LIMIT),
        name=f"attn_d{d}",
    )(qv, kv, kv, vv, vv)


def _outproj_kernel(x_ref, hm_ref, o1_ref, o4_ref, o16_ref, l1_ref, l4_ref, l16_ref, az_ref, w_ref, ex_ref,
                    out_ref, o4n, o16n, l4n, l16n, ha_scr):
    tm = x_ref.shape[0]
    for d, o_ref, l_ref, o_nat, l_nat in ((4, o4_ref, l4_ref, o4n, l4n), (16, o16_ref, l16_ref, o16n, l16n)):
        for r in range(d):
            l_nat[pl.ds(r, tm // d, stride=d), :] = l_ref[:, r * LANES:(r + 1) * LANES]
            for sl in range(A_WIDTH // LANES):
                c0 = r * A_WIDTH + sl * LANES
                o_nat[sl, pl.ds(r, tm // d, stride=d), :] = o_ref[:, c0:c0 + LANES].astype(F32)

    l1, l2, l3 = l1_ref[...], l4n[...], l16n[...]
    mx = jnp.maximum(jnp.maximum(l1, l2), l3)
    e1, e2, e3 = jnp.exp(l1 - mx), jnp.exp(l2 - mx), jnp.exp(l3 - mx)
    tot = e1 + e2 + e3
    ex = ex_ref[...]

    def expand(w):
        hi, lo = _split2(w)
        return _dot(hi, ex) + _dot(lo, ex)

    wts = (expand(e1 / tot), expand(e2 / tot), expand(e3 / tot))
    for sl in range(A_WIDTH // LANES):
        cols = slice(sl * LANES, (sl + 1) * LANES)
        acc = (wts[0][:, cols] * o1_ref[:, cols].astype(F32) + wts[1][:, cols] * o4n[sl]
               + wts[2][:, cols] * o16n[sl])
        zz = az_ref[:, cols].astype(F32)
        ha_scr[:, cols] = (acc * (zz * _sigmoid(zz))).astype(BF16)

    y = _dot(hm_ref[...], w_ref[0:M_WIDTH, :]) + _dot(ha_scr[...], w_ref[M_WIDTH:D_MIX, :])
    out_ref[...] = x_ref[...] + y


def _outproj(x2d, hm, outs, lses, az, w_out, expand):
    T = x2d.shape[0]
    tm = TM_PROJ
    row = lambda i: (i, 0)
    const = lambda i: (0, 0)
    dils = [d for _, d in DILATED_PATTERNS]
    return pl.pallas_call(
        _outproj_kernel,
        grid=(T // tm,),
        in_specs=[pl.BlockSpec((tm, D_MODEL), row), pl.BlockSpec((tm, M_WIDTH), row)]
        + [pl.BlockSpec((tm // d, d * A_WIDTH), row) for d in dils]
        + [pl.BlockSpec((tm // d, d * LANES), row) for d in dils]
        + [pl.BlockSpec((tm, A_WIDTH), row), pl.BlockSpec((D_MIX, D_MODEL), const),
           pl.BlockSpec((LANES, A_WIDTH), const)],
        out_specs=pl.BlockSpec((tm, D_MODEL), row),
        out_shape=jax.ShapeDtypeStruct((T, D_MODEL), F32),
        scratch_shapes=[
            pltpu.VMEM((A_WIDTH // LANES, tm, LANES), F32),
            pltpu.VMEM((A_WIDTH // LANES, tm, LANES), F32),
            pltpu.VMEM((tm, LANES), F32),
            pltpu.VMEM((tm, LANES), F32),
            pltpu.VMEM((tm, A_WIDTH), BF16),
        ],
        compiler_params=pltpu.CompilerParams(dimension_semantics=("arbitrary",), vmem_limit_bytes=VMEM_LIMIT),
        name="outproj",
    )(x2d, hm, *outs, *lses, az, w_out, expand)


def _constants():
    idx = jnp.arange(MXU_TILE)
    bd = (idx[:, None] // A_HEAD_DIM == idx[None, :] // A_HEAD_DIM).astype(BF16)
    t = jnp.arange(M_CHUNK)
    tril = (t[None, :] <= t[:, None]).astype(BF16)
    expand = (jnp.arange(LANES)[:, None] == jnp.arange(A_WIDTH)[None, :] // A_HEAD_DIM).astype(BF16)
    return bd, tril, expand


def kernel(x, norm_g, w_in, gate_b, conv_w, conv_b, m_norm_g, q_norm_g, k_norm_g, w_out):
    B, S, D = x.shape
    depth = norm_g.shape[0]
    assert D == D_MODEL and S % M_CHUNK == 0 and S % TM_PROJ == 0 and B % M_SEQS_PER_STEP == 0
    for window, dilation in DILATED_PATTERNS:
        assert window == dilation * A_BLOCK and S % (dilation * A_BLOCK) == 0 and TM_PROJ % (16 * dilation) == 0
    bd, tril, expand = _constants()
    n_gate0 = 2 * M_QK_WIDTH + 3 * M_WIDTH
    x2d = x.reshape(B * S, D)
    for l in range(depth):
        w = w_in[l]
        w_all = jnp.concatenate(
            [w[:, :n_gate0], w[:, n_gate0 + 2 * M_HEADS:],
             jnp.pad(w[:, n_gate0:n_gate0 + 2 * M_HEADS], ((0, 0), (0, GATE_PAD - 2 * M_HEADS)))],
            axis=1).astype(BF16)
        gq = (jnp.tile(q_norm_g[l], A_HEADS) * (A_HEAD_DIM ** -0.5 * LOG2E))[None, :]
        gk = jnp.tile(k_norm_g[l], A_HEADS)[None, :]
        gb = jnp.pad(gate_b[l], (0, GATE_PAD - 2 * M_HEADS))[None, :]
        qk, mv, mo, mz, az, gates, *qkv_views = _inproj(x2d, norm_g[l][None, :], w_all, bd, gq, gk,
                                                        conv_w[l], conv_b[l][None, :], gb, tril, S)
        hm = _mlstm(qk, mv, mo, mz, gates, m_norm_g[l][None, :], B, S)
        outs, lses = [], []
        for n, (_, dilation) in enumerate(DILATED_PATTERNS):
            o, lse = _attn_pattern(*qkv_views[3 * n:3 * n + 3], B, S, dilation)
            outs.append(o)
            lses.append(lse)
        x2d = _outproj(x2d, hm, outs, lses, az, w_out[l].astype(BF16), expand)
    return x2d.reshape(B, S, D)
```

```python
import functools

import jax
import jax.numpy as jnp
from jax import lax
from jax.experimental import pallas as pl
from jax.experimental.pallas import tpu as pltpu

F32 = jnp.float32
BF16 = jnp.bfloat16

EPS = 1e-6
D_MODEL = 1024
M_HEADS = 4
M_V_DIM = 256
M_QK_DIM = 128
M_QK_WIDTH = M_HEADS * M_QK_DIM
M_WIDTH = M_HEADS * M_V_DIM
CONV_K = 4
A_HEAD_DIM = 64
A_HEADS = 8
A_WIDTH = A_HEADS * A_HEAD_DIM
A_BLOCK = 128
DILATED_PATTERNS = ((128, 1), (512, 4), (2048, 16))
D_MIX = M_WIDTH + A_WIDTH

LANES = 128
MXU_TILE = 256
GATE_PAD = LANES
N_MAIN = 2 * M_QK_WIDTH + 3 * M_WIDTH + 4 * A_WIDTH
NEG_BIG = -1e30
LOG2E = 1.4426950408889634
LN2 = 0.6931471805599453

V7X_VMEM_BYTES = 64 * 1024 * 1024
VMEM_LIMIT = V7X_VMEM_BYTES * 7 // 8

TM_PROJ = 512
A_ROWS_PER_STEP = 512
M_CHUNK = 256
M_SEQS_PER_STEP = 2


def _sigmoid(x):
    return 1.0 / (1.0 + jnp.exp(-x))


def _split3(x):
    hi = x.astype(BF16)
    r1 = x - hi.astype(F32)
    mid = r1.astype(BF16)
    lo = (r1 - mid.astype(F32)).astype(BF16)
    return hi, mid, lo


def _split2(x):
    hi = x.astype(BF16)
    return hi, (x - hi.astype(F32)).astype(BF16)


def _dot(a, b):
    return jnp.dot(a, b, preferred_element_type=F32)


def _inproj_kernel(tiles_per_seq, x_ref, g_ref, w_ref, bd_ref, gq_ref, gk_ref, cw_ref, cb_ref, gb_ref, tril_ref,
                   qk_ref, v_ref, o_ref, z_ref, az_ref, gate_ref,
                   aq_ref, ak_ref, av_ref, aq4_ref, ak4_ref, av4_ref, aq16_ref, ak16_ref, av16_ref, rel, cbuf, h_ref):
    tm = x_ref.shape[0]
    first = (pl.program_id(0) % tiles_per_seq) == 0

    @pl.when(first)
    def _():
        cbuf[0:8, :] = jnp.zeros((8, cbuf.shape[1]), F32)

    @pl.when(jnp.logical_not(first))
    def _():
        cbuf[0:8, :] = cbuf[tm:tm + 8, :]

    x = x_ref[...]
    h_ref[...] = (x * g_ref[...]).astype(BF16)
    rs = lax.rsqrt(jnp.mean(x * x, axis=-1, keepdims=True) + EPS)

    def proj(c0, c1):
        return _dot(h_ref[...], w_ref[:, c0:c1]) * rs

    def head_norm(a, gain_ref):
        sq = (a * a).astype(BF16)
        bd = bd_ref[...]
        half = bd.shape[0]
        ss = jnp.concatenate([_dot(sq[:, s:s + half], bd) for s in range(0, A_WIDTH, half)], axis=1)
        return a * lax.rsqrt(ss * (1.0 / A_HEAD_DIM) + EPS) * gain_ref[...]

    def emit_views(val, nat_ref, view_refs, rel):
        nat_ref[...] = val.astype(BF16)
        for sl in range(A_WIDTH // LANES):
            rel[sl] = val[:, sl * LANES:(sl + 1) * LANES]
        for d, ref in view_refs:
            for r in range(d):
                for sl in range(A_WIDTH // LANES):
                    c0 = r * A_WIDTH + sl * LANES
                    ref[:, c0:c0 + LANES] = rel[sl, pl.ds(r, tm // d, stride=d), :].astype(BF16)

    c_qk, c_v, c_o, c_z = 0, 2 * M_QK_WIDTH, 2 * M_QK_WIDTH + M_WIDTH, 2 * M_QK_WIDTH + 2 * M_WIDTH
    c_aq = c_z + M_WIDTH
    c_ak, c_av, c_az, c_gate = c_aq + A_WIDTH, c_aq + 2 * A_WIDTH, c_aq + 3 * A_WIDTH, c_aq + 4 * A_WIDTH

    def conv_silu(raw):
        cbuf[8:8 + tm, :] = raw
        conv = cb_ref[...] + cw_ref[0:1, :] * cbuf[5:5 + tm, :]
        for j in range(1, CONV_K):
            conv = conv + cw_ref[j:j + 1, :] * cbuf[5 + j:5 + j + tm, :]
        act = conv * _sigmoid(conv)
        qk_ref[:, 0:M_QK_WIDTH] = (act[:, 0:M_QK_WIDTH] * (M_QK_DIM ** -0.5)).astype(BF16)
        qk_ref[:, M_QK_WIDTH:] = act[:, M_QK_WIDTH:].astype(BF16)

    def gate_logs(raw):
        g = raw + gb_ref[...]
        logf = (jnp.minimum(g, 0.0) - jnp.log1p(jnp.exp(-jnp.abs(g)))) * LOG2E
        hi, mid, lo = _split3(logf)
        tril = tril_ref[...]
        L = tril.shape[0]
        lane = lax.broadcasted_iota(jnp.int32, (L, LANES), 1)
        for c0 in range(0, tm, L):
            cum = _dot(tril, hi[c0:c0 + L]) + _dot(tril, mid[c0:c0 + L]) + _dot(tril, lo[c0:c0 + L])
            gate_ref[c0:c0 + L, :] = jnp.where(lane < M_HEADS, g[c0:c0 + L] * LOG2E, cum)

    def store(ref):
        def epilogue(raw):
            ref[...] = raw.astype(BF16)
        return epilogue

    stages = (
        ((c_aq, c_ak), lambda r: emit_views(head_norm(r, gq_ref), aq_ref, ((4, aq4_ref), (16, aq16_ref)), rel.at[0])),
        ((c_v, c_o), store(v_ref)),
        ((c_ak, c_av), lambda r: emit_views(head_norm(r, gk_ref), ak_ref, ((4, ak4_ref), (16, ak16_ref)), rel.at[1])),
        ((c_o, c_z), store(o_ref)),
        ((c_av, c_az), lambda r: emit_views(r, av_ref, ((4, av4_ref), (16, av16_ref)), rel.at[2])),
        ((c_z, c_aq), store(z_ref)),
        ((c_qk, c_v), conv_silu),
        ((c_az, c_gate), store(az_ref)),
        ((c_gate, c_gate + GATE_PAD), gate_logs),
    )
    for cols, epilogue in stages:
        epilogue(proj(*cols))


def _inproj(x2d, norm_g, w_all, bd, gq, gk, conv_w, conv_b, gate_b, tril, S):
    T = x2d.shape[0]
    tm = TM_PROJ
    assert tm % tril.shape[0] == 0
    const = lambda i: (0, 0)
    row = lambda i: (i, 0)
    widths = (2 * M_QK_WIDTH, M_WIDTH, M_WIDTH, M_WIDTH, A_WIDTH)
    out_shape = [jax.ShapeDtypeStruct((T, w), BF16) for w in widths] + [jax.ShapeDtypeStruct((T, GATE_PAD), F32)]
    out_specs = [pl.BlockSpec((tm, w), row) for w in widths] + [pl.BlockSpec((tm, GATE_PAD), row)]
    for _, d in DILATED_PATTERNS:
        out_shape += [jax.ShapeDtypeStruct((T // d, d * A_WIDTH), BF16)] * 3
        out_specs += [pl.BlockSpec((tm // d, d * A_WIDTH), row)] * 3
    return pl.pallas_call(
        functools.partial(_inproj_kernel, S // tm),
        grid=(T // tm,),
        in_specs=[
            pl.BlockSpec((tm, D_MODEL), row),
            pl.BlockSpec((1, D_MODEL), const),
            pl.BlockSpec((D_MODEL, N_MAIN + GATE_PAD), const, pipeline_mode=pl.Buffered(1)),
            pl.BlockSpec(bd.shape, const),
            pl.BlockSpec((1, A_WIDTH), const),
            pl.BlockSpec((1, A_WIDTH), const),
            pl.BlockSpec((CONV_K, 2 * M_QK_WIDTH), const),
            pl.BlockSpec((1, 2 * M_QK_WIDTH), const),
            pl.BlockSpec((1, GATE_PAD), const),
            pl.BlockSpec(tril.shape, const),
        ],
        out_specs=out_specs,
        out_shape=out_shape,
        scratch_shapes=[
            pltpu.VMEM((3, A_WIDTH // LANES, tm, LANES), F32),
            pltpu.VMEM((tm + 8, 2 * M_QK_WIDTH), F32),
            pltpu.VMEM((tm, D_MODEL), BF16),
        ],
        compiler_params=pltpu.CompilerParams(dimension_semantics=("arbitrary",), vmem_limit_bytes=VMEM_LIMIT),
        name="inproj",
    )(x2d, norm_g, w_all, bd, gq, gk, conv_w, conv_b, gate_b, tril)


def _mlstm_kernel(qk_ref, v_ref, o_ref, z_ref, gate_ref, mg_ref, out_ref, c_ref, n_ref, m_ref, cbias_ref):
    nseq, L = qk_ref.shape[0], qk_ref.shape[1]

    @pl.when(pl.program_id(1) == 0)
    def _():
        c_ref[...] = jnp.zeros(c_ref.shape, F32)
        n_ref[...] = jnp.zeros(n_ref.shape, F32)
        m_ref[...] = jnp.zeros(m_ref.shape, F32)

    t_idx = lax.broadcasted_iota(jnp.int32, (L, L), 0)
    s_idx = lax.broadcasted_iota(jnp.int32, (L, L), 1)
    cbias_ref[...] = jnp.where(s_idx <= t_idx, 0.0, NEG_BIG)
    lane = lax.broadcasted_iota(jnp.int32, (L, LANES), 1)
    ones = jnp.ones((L, LANES), BF16)

    heads = range(M_HEADS)
    for i in range(nseq):
        unit = [i * M_HEADS + h for h in heads]
        q = [qk_ref[i, :, h * M_QK_DIM:(h + 1) * M_QK_DIM] for h in heads]
        k = [qk_ref[i, :, M_QK_WIDTH + h * M_QK_DIM:M_QK_WIDTH + (h + 1) * M_QK_DIM] for h in heads]
        v = [v_ref[i, :, h * M_V_DIM:(h + 1) * M_V_DIM] for h in heads]
        m_prev = [m_ref[u] for u in unit]
        c_prev = [c_ref[u] for u in unit]
        n_prev = [n_ref[u] for u in unit]

        qk = [lax.dot_general(q[h], k[h], (((1,), (1,)), ((), ())), preferred_element_type=F32) for h in heads]
        qc = [_dot(q[h], jnp.concatenate([c_prev[h], n_prev[h]], axis=1).astype(BF16)) for h in heads]

        gb = gate_ref[i]
        gbt = gb.T[0:2 * M_HEADS, :]
        b_row = [gbt[M_HEADS + h:M_HEADS + h + 1, :] for h in heads]
        a_row = [gbt[h:h + 1, :] - b_row[h] for h in heads]

        for h in heads:
            b_last = b_row[h][:, L - 1:L]
            log_state = b_last + a_row[h]
            m_new = jnp.maximum(b_last + m_prev[h], jnp.max(log_state, axis=1, keepdims=True))
            decay = jnp.exp2(b_last + m_prev[h] - m_new)
            kw = k[h].astype(F32).T * jnp.exp2(log_state - m_new)
            c_ref[unit[h]] = decay * c_prev[h] + _dot(kw.astype(BF16), v[h])
            n_ref[unit[h]] = decay * n_prev[h] + jnp.sum(kw, axis=1, keepdims=True)
            m_ref[unit[h]] = m_new

        s, m_t, w_inter = [], [], []
        for h in heads:
            b_col = jnp.sum(jnp.where(lane == M_HEADS + h, gb, 0.0), axis=1, keepdims=True)
            log_intra = (b_col + a_row[h]) + cbias_ref[...]
            log_inter = b_col + m_prev[h]
            m_t.append(jnp.maximum(log_inter, jnp.max(log_intra, axis=1, keepdims=True)))
            w_inter.append(jnp.exp2(log_inter - m_t[h]))
            s.append((qk[h] * jnp.exp2(log_intra - m_t[h])).astype(BF16))
        sv = [_dot(s[h], jnp.concatenate([v[h], ones], axis=1)) for h in heads]

        for h in heads:
            num = sv[h][:, :M_V_DIM] + w_inter[h] * qc[h][:, :M_V_DIM]
            den = sv[h][:, M_V_DIM:] + w_inter[h] * qc[h][:, M_V_DIM:]
            den = jnp.maximum(jnp.abs(den), jnp.exp2(-m_t[h]))
            hid = num / jnp.concatenate([den] * (M_V_DIM // LANES), axis=1)
            sl = slice(h * M_V_DIM, (h + 1) * M_V_DIM)
            hg = _sigmoid(o_ref[i, :, sl].astype(F32)) * hid
            hn = hg * lax.rsqrt(jnp.mean(hg * hg, axis=-1, keepdims=True) + EPS) * mg_ref[:, sl]
            zz = z_ref[i, :, sl].astype(F32)
            out_ref[i, :, sl] = (hn * (zz * _sigmoid(zz))).astype(BF16)


def _mlstm(qk, v, o, z, gates, m_norm_g, B, S):
    L = M_CHUNK
    nseq = M_SEQS_PER_STEP
    const = lambda g, s: (0, 0)
    seqs = lambda g, s: (g, 0, s, 0)
    operands = [a.reshape(B // nseq, nseq, S, a.shape[-1]) for a in (qk, v, o, z, gates)]
    out = pl.pallas_call(
        _mlstm_kernel,
        grid=(B // nseq, S // L),
        in_specs=[pl.BlockSpec((None, nseq, L, a.shape[-1]), seqs) for a in operands]
        + [pl.BlockSpec((1, M_WIDTH), const)],
        out_specs=pl.BlockSpec((None, nseq, L, M_WIDTH), seqs),
        out_shape=jax.ShapeDtypeStruct((B // nseq, nseq, S, M_WIDTH), BF16),
        scratch_shapes=[
            pltpu.VMEM((nseq * M_HEADS, M_QK_DIM, M_V_DIM), F32),
            pltpu.VMEM((nseq * M_HEADS, M_QK_DIM, LANES), F32),
            pltpu.VMEM((nseq * M_HEADS, 1, 1), F32),
            pltpu.VMEM((L, L), F32),
        ],
        compiler_params=pltpu.CompilerParams(dimension_semantics=("arbitrary", "arbitrary"),
                                             vmem_limit_bytes=VMEM_LIMIT),
        name="mlstm",
    )(*operands, m_norm_g)
    return out.reshape(B * S, M_WIDTH)


def _attn_kernel(q_ref, kp_ref, kc_ref, vp_ref, vc_ref, o_ref, lse_ref, bias_ref, bias0_ref):
    TQ = q_ref.shape[0]
    i = pl.program_id(2)
    qi = lax.broadcasted_iota(jnp.int32, (2 * A_BLOCK, 2 * A_BLOCK), 0) % A_BLOCK
    kc = lax.broadcasted_iota(jnp.int32, (2 * A_BLOCK, 2 * A_BLOCK), 1)
    band = (kc >= qi) & (kc <= qi + A_BLOCK)
    bias_ref[...] = jnp.where(band, 0.0, NEG_BIG)
    bias0_ref[...] = jnp.where(band & (kc >= jnp.where(i > 0, 0, A_BLOCK)), 0.0, NEG_BIG)
    lane = lax.broadcasted_iota(jnp.int32, (A_BLOCK, LANES), 1)
    low_half = lane < A_HEAD_DIM
    zero = jnp.zeros((A_BLOCK, LANES), BF16)
    ones = jnp.ones((2 * A_BLOCK, LANES), BF16)

    n_res = q_ref.shape[1] // A_WIDTH
    for res, j in [(res, j) for res in range(n_res) for j in range(TQ // A_BLOCK)]:
        bias = bias0_ref if j == 0 else bias_ref
        rows = slice(j * A_BLOCK, (j + 1) * A_BLOCK)
        mx_tile = jnp.zeros((A_BLOCK, LANES), F32)
        den_tile = jnp.ones((A_BLOCK, LANES), F32)
        for p in range(A_HEADS // 2):
            cols = slice(res * A_WIDTH + p * LANES, res * A_WIDTH + (p + 1) * LANES)
            if j == 0:
                kk = jnp.concatenate([kp_ref[:, cols], kc_ref[0:A_BLOCK, cols]], axis=0)
                vv = jnp.concatenate([vp_ref[:, cols], vc_ref[0:A_BLOCK, cols]], axis=0)
            else:
                kk = kc_ref[(j - 1) * A_BLOCK:(j + 1) * A_BLOCK, cols]
                vv = vc_ref[(j - 1) * A_BLOCK:(j + 1) * A_BLOCK, cols]
            q2 = q_ref[rows, cols]
            qs = jnp.concatenate([jnp.where(low_half, q2, zero), jnp.where(low_half, zero, q2)], axis=0)
            s = lax.dot_general(qs, kk, (((1,), (1,)), ((), ())), preferred_element_type=F32)
            s = s + bias[...]
            mx = jnp.max(s, axis=1, keepdims=True)
            pr = jnp.exp2(s - mx).astype(BF16)
            pv = _dot(pr, jnp.concatenate([vv, ones], axis=1))
            den = pv[:, LANES:]
            out = pv[:, :LANES] / den
            o_ref[rows, cols] = jnp.where(low_half, out[0:A_BLOCK], out[A_BLOCK:]).astype(BF16)
            mx_tile = jnp.where(lane == 2 * p, mx[0:A_BLOCK], jnp.where(lane == 2 * p + 1, mx[A_BLOCK:], mx_tile))
            den_tile = jnp.where(lane == 2 * p, den[0:A_BLOCK], jnp.where(lane == 2 * p + 1, den[A_BLOCK:], den_tile))
        lse_ref[rows, res * LANES:(res + 1) * LANES] = mx_tile * LN2 + jnp.log(den_tile)


def _attn_pattern(qv, kv, vv, B, S, dilation):
    d = dilation
    L = S // d
    TQ = min(A_ROWS_PER_STEP, L)
    nq = L // TQ
    r = TQ // A_BLOCK
    n_res = min(d, A_ROWS_PER_STEP // TQ)
    cur = lambda b, rr, i: (b * nq + i, rr)
    prev = lambda b, rr, i: (b * (L // A_BLOCK) + jnp.maximum(i * r - 1, 0), rr)
    return pl.pallas_call(
        _attn_kernel,
        grid=(B, d // n_res, nq),
        in_specs=[
            pl.BlockSpec((TQ, n_res * A_WIDTH), cur),
            pl.BlockSpec((A_BLOCK, n_res * A_WIDTH), prev),
            pl.BlockSpec((TQ, n_res * A_WIDTH), cur),
            pl.BlockSpec((A_BLOCK, n_res * A_WIDTH), prev),
            pl.BlockSpec((TQ, n_res * A_WIDTH), cur),
        ],
        out_specs=[
            pl.BlockSpec((TQ, n_res * A_WIDTH), cur),
            pl.BlockSpec((TQ, n_res * LANES), cur),
        ],
        out_shape=[
            jax.ShapeDtypeStruct((B * L, d * A_WIDTH), BF16),
            jax.ShapeDtypeStruct((B * L, d * LANES), F32),
        ],
        scratch_shapes=[
            pltpu.VMEM((2 * A_BLOCK, 2 * A_BLOCK), F32),
            pltpu.VMEM((2 * A_BLOCK, 2 * A_BLOCK), F32),
        ],
        compiler_params=pltpu.CompilerParams(dimension_semantics=("arbitrary", "arbitrary", "arbitrary"),
                                             vmem_limit_bytes=VMEM_LIMIT),
        name=f"attn_d{d}",
    )(qv, kv, kv, vv, vv)


def _outproj_kernel(x_ref, hm_ref, o1_ref, o4_ref, o16_ref, l1_ref, l4_ref, l16_ref, az_ref, w_ref, ex_ref,
                    out_ref, o4n, o16n, l4n, l16n, ha_scr):
    tm = x_ref.shape[0]
    out_ref[...] = x_ref[...] + _dot(hm_ref[...], w_ref[0:M_WIDTH, :])
    for d, o_ref, l_ref, o_nat, l_nat in ((4, o4_ref, l4_ref, o4n, l4n), (16, o16_ref, l16_ref, o16n, l16n)):
        for r in range(d):
            l_nat[pl.ds(r, tm // d, stride=d), :] = l_ref[:, r * LANES:(r + 1) * LANES]
            for sl in range(A_WIDTH // LANES):
                c0 = r * A_WIDTH + sl * LANES
                o_nat[sl, pl.ds(r, tm // d, stride=d), :] = o_ref[:, c0:c0 + LANES].astype(F32)

    l1, l2, l3 = l1_ref[...], l4n[...], l16n[...]
    mx = jnp.maximum(jnp.maximum(l1, l2), l3)
    e1, e2, e3 = jnp.exp(l1 - mx), jnp.exp(l2 - mx), jnp.exp(l3 - mx)
    tot = e1 + e2 + e3
    lane = lax.broadcasted_iota(jnp.int32, (tm, LANES), 1)
    packed = None
    for n, w in enumerate((e1 / tot, e2 / tot, e3 / tot)):
        hi = w.astype(BF16).astype(F32)
        for part in (hi, w - hi):
            slot = 2 * n + (0 if part is hi else 1)
            packed = part if packed is None else jnp.where(lane < A_HEADS * slot, packed,
                                                             pltpu.roll(part, A_HEADS * slot, axis=1))
    wexp = _dot(packed.astype(BF16), ex_ref[...])
    for sl in range(A_WIDTH // LANES):
        cols = slice(sl * LANES, (sl + 1) * LANES)
        pats = (o1_ref[:, cols].astype(F32), o4n[sl], o16n[sl])
        wcol = lambda n: wexp[:, n * A_WIDTH + sl * LANES:n * A_WIDTH + (sl + 1) * LANES]
        acc = wcol(0) * pats[0] + wcol(1) * pats[1] + wcol(2) * pats[2]
        zz = az_ref[:, cols].astype(F32)
        ha_scr[:, cols] = (acc * (zz * _sigmoid(zz))).astype(BF16)

    out_ref[...] += _dot(ha_scr[...], w_ref[M_WIDTH:D_MIX, :])


def _outproj(x2d, hm, outs, lses, az, w_out, expand):
    T = x2d.shape[0]
    tm = TM_PROJ
    row = lambda i: (i, 0)
    const = lambda i: (0, 0)
    dils = [d for _, d in DILATED_PATTERNS]
    return pl.pallas_call(
        _outproj_kernel,
        grid=(T // tm,),
        in_specs=[pl.BlockSpec((tm, D_MODEL), row), pl.BlockSpec((tm, M_WIDTH), row)]
        + [pl.BlockSpec((tm // d, d * A_WIDTH), row) for d in dils]
        + [pl.BlockSpec((tm // d, d * LANES), row) for d in dils]
        + [pl.BlockSpec((tm, A_WIDTH), row), pl.BlockSpec((D_MIX, D_MODEL), const),
           pl.BlockSpec(expand.shape, const)],
        out_specs=pl.BlockSpec((tm, D_MODEL), row),
        out_shape=jax.ShapeDtypeStruct((T, D_MODEL), F32),
        scratch_shapes=[
            pltpu.VMEM((A_WIDTH // LANES, tm, LANES), F32),
            pltpu.VMEM((A_WIDTH // LANES, tm, LANES), F32),
            pltpu.VMEM((tm, LANES), F32),
            pltpu.VMEM((tm, LANES), F32),
            pltpu.VMEM((tm, A_WIDTH), BF16),
        ],
        compiler_params=pltpu.CompilerParams(dimension_semantics=("arbitrary",), vmem_limit_bytes=VMEM_LIMIT),
        name="outproj",
    )(x2d, hm, *outs, *lses, az, w_out, expand)


def _constants():
    idx = jnp.arange(MXU_TILE)
    bd = (idx[:, None] // A_HEAD_DIM == idx[None, :] // A_HEAD_DIM).astype(BF16)
    t = jnp.arange(M_CHUNK)
    tril = (t[None, :] <= t[:, None]).astype(BF16)
    row = jnp.arange(LANES)[:, None]
    col = jnp.arange(len(DILATED_PATTERNS) * A_WIDTH)[None, :]
    expand = ((row // (2 * A_HEADS) == col // A_WIDTH) & (row % A_HEADS == (col % A_WIDTH) // A_HEAD_DIM)
              & (row < 2 * A_HEADS * len(DILATED_PATTERNS))).astype(BF16)
    return bd, tril, expand


def kernel(x, norm_g, w_in, gate_b, conv_w, conv_b, m_norm_g, q_norm_g, k_norm_g, w_out):
    B, S, D = x.shape
    depth = norm_g.shape[0]
    assert D == D_MODEL and S % M_CHUNK == 0 and S % TM_PROJ == 0 and B % M_SEQS_PER_STEP == 0
    for window, dilation in DILATED_PATTERNS:
        assert window == dilation * A_BLOCK and S % (dilation * A_BLOCK) == 0 and TM_PROJ % (16 * dilation) == 0
    bd, tril, expand = _constants()
    n_gate0 = 2 * M_QK_WIDTH + 3 * M_WIDTH
    x2d = x.reshape(B * S, D)
    for l in range(depth):
        w = w_in[l]
        w_all = jnp.concatenate(
            [w[:, :n_gate0], w[:, n_gate0 + 2 * M_HEADS:],
             jnp.pad(w[:, n_gate0:n_gate0 + 2 * M_HEADS], ((0, 0), (0, GATE_PAD - 2 * M_HEADS)))],
            axis=1).astype(BF16)
        gq = (jnp.tile(q_norm_g[l], A_HEADS) * (A_HEAD_DIM ** -0.5 * LOG2E))[None, :]
        gk = jnp.tile(k_norm_g[l], A_HEADS)[None, :]
        gb = jnp.pad(gate_b[l], (0, GATE_PAD - 2 * M_HEADS))[None, :]
        qk, mv, mo, mz, az, gates, *qkv_views = _inproj(x2d, norm_g[l][None, :], w_all, bd, gq, gk,
                                                        conv_w[l], conv_b[l][None, :], gb, tril, S)
        hm = _mlstm(qk, mv, mo, mz, gates, m_norm_g[l][None, :], B, S)
        outs, lses = [], []
        for n, (_, dilation) in enumerate(DILATED_PATTERNS):
            o, lse = _attn_pattern(*qkv_views[3 * n:3 * n + 3], B, S, dilation)
            outs.append(o)
            lses.append(lse)
        x2d = _outproj(x2d, hm, outs, lses, az, w_out[l].astype(BF16), expand)
    return x2d.reshape(B, S, D)
```

```python
import functools

import jax
import jax.numpy as jnp
from jax import lax
from jax.experimental import pallas as pl
from jax.experimental.pallas import tpu as pltpu

F32 = jnp.float32
BF16 = jnp.bfloat16

EPS = 1e-6
D_MODEL = 1024
M_HEADS = 4
M_V_DIM = 256
M_QK_DIM = 128
M_QK_WIDTH = M_HEADS * M_QK_DIM
M_WIDTH = M_HEADS * M_V_DIM
CONV_K = 4
A_HEAD_DIM = 64
A_HEADS = 8
A_WIDTH = A_HEADS * A_HEAD_DIM
A_BLOCK = 128
DILATED_PATTERNS = ((128, 1), (512, 4), (2048, 16))
D_MIX = M_WIDTH + A_WIDTH

LANES = 128
MXU_TILE = 256
GATE_PAD = LANES
N_MAIN = 2 * M_QK_WIDTH + 3 * M_WIDTH + 4 * A_WIDTH
NEG_BIG = -1e30
LOG2E = 1.4426950408889634
LN2 = 0.6931471805599453

V7X_VMEM_BYTES = 64 * 1024 * 1024
VMEM_LIMIT = V7X_VMEM_BYTES * 7 // 8

TM_PROJ = 512
A_ROWS_PER_STEP = 512
M_CHUNK = 256
M_SEQS_PER_STEP = 2


def _sigmoid(x):
    return 1.0 / (1.0 + jnp.exp(-x))


def _split3(x):
    hi = x.astype(BF16)
    r1 = x - hi.astype(F32)
    mid = r1.astype(BF16)
    lo = (r1 - mid.astype(F32)).astype(BF16)
    return hi, mid, lo


def _split2(x):
    hi = x.astype(BF16)
    return hi, (x - hi.astype(F32)).astype(BF16)


def _dot(a, b):
    return jnp.dot(a, b, preferred_element_type=F32)


def _inproj_kernel(tiles_per_seq, x_ref, g_ref, w_ref, bd_ref, gq_ref, gk_ref, cw_ref, cb_ref, gb_ref, tril_ref,
                   q_ref, v_ref, o_ref, z_ref, az_ref, gate_ref, kt_ref,
                   aq_ref, ak_ref, av_ref, aq4_ref, ak4_ref, av4_ref, aq16_ref, ak16_ref, av16_ref, rel, rel4, cbuf, h_ref):
    tm = x_ref.shape[0]
    first = (pl.program_id(0) % tiles_per_seq) == 0

    @pl.when(first)
    def _():
        cbuf[0:8, :] = jnp.zeros((8, cbuf.shape[1]), F32)

    @pl.when(jnp.logical_not(first))
    def _():
        cbuf[0:8, :] = cbuf[tm:tm + 8, :]

    x = x_ref[...]
    h_ref[...] = (x * g_ref[...]).astype(BF16)
    rs = lax.rsqrt(jnp.mean(x * x, axis=-1, keepdims=True) + EPS)

    def proj(c0, c1):
        return _dot(h_ref[...], w_ref[:, c0:c1]) * rs

    def head_norm(a, gain_ref):
        sq = (a * a).astype(BF16)
        bd = bd_ref[...]
        half = bd.shape[0]
        ss = jnp.concatenate([_dot(sq[:, s:s + half], bd) for s in range(0, A_WIDTH, half)], axis=1)
        return a * lax.rsqrt(ss * (1.0 / A_HEAD_DIM) + EPS) * gain_ref[...]

    def emit_views(val, nat_ref, v4_ref, v16_ref, rel, rel4):
        nsl = A_WIDTH // LANES
        nat_ref[...] = val.astype(BF16)
        for sl in range(nsl):
            rel[sl] = val[:, sl * LANES:(sl + 1) * LANES]
        for r4 in range(4):
            for sl in range(nsl):
                blk = rel[sl, pl.ds(r4, tm // 4, stride=4), :]
                rel4[r4 * nsl + sl] = blk
                v4_ref[:, r4 * A_WIDTH + sl * LANES:r4 * A_WIDTH + (sl + 1) * LANES] = blk.astype(BF16)
        for r in range(16):
            for sl in range(nsl):
                c0 = r * A_WIDTH + sl * LANES
                v16_ref[:, c0:c0 + LANES] = rel4[(r % 4) * nsl + sl, pl.ds(r // 4, tm // 16, stride=4), :].astype(BF16)

    c_qk, c_v, c_o, c_z = 0, 2 * M_QK_WIDTH, 2 * M_QK_WIDTH + M_WIDTH, 2 * M_QK_WIDTH + 2 * M_WIDTH
    c_aq = c_z + M_WIDTH
    c_ak, c_av, c_az, c_gate = c_aq + A_WIDTH, c_aq + 2 * A_WIDTH, c_aq + 3 * A_WIDTH, c_aq + 4 * A_WIDTH

    def conv_silu(raw):
        cbuf[8:8 + tm, :] = raw
        conv = cb_ref[...] + cw_ref[0:1, :] * cbuf[5:5 + tm, :]
        for j in range(1, CONV_K):
            conv = conv + cw_ref[j:j + 1, :] * cbuf[5 + j:5 + j + tm, :]
        act = conv * _sigmoid(conv)
        q_ref[...] = (act[:, 0:M_QK_WIDTH] * (M_QK_DIM ** -0.5)).astype(BF16)
        chunk = kt_ref.shape[-1]
        for c in range(tm // chunk):
            for hd in range(M_HEADS):
                c0 = M_QK_WIDTH + hd * M_QK_DIM
                kt_ref[c, hd] = act[c * chunk:(c + 1) * chunk, c0:c0 + M_QK_DIM].T.astype(BF16)

    def gate_logs(raw):
        g = raw + gb_ref[...]
        logf = (jnp.minimum(g, 0.0) - jnp.log1p(jnp.exp(-jnp.abs(g)))) * LOG2E
        hi, mid, lo = _split3(logf)
        tril = tril_ref[...]
        L = tril.shape[0]
        lane = lax.broadcasted_iota(jnp.int32, (L, LANES), 1)
        for c0 in range(0, tm, L):
            cum = _dot(tril, hi[c0:c0 + L]) + _dot(tril, mid[c0:c0 + L]) + _dot(tril, lo[c0:c0 + L])
            gate_ref[c0:c0 + L, :] = jnp.where(lane < M_HEADS, g[c0:c0 + L] * LOG2E, cum)

    def store(ref):
        def epilogue(raw):
            ref[...] = raw.astype(BF16)
        return epilogue

    stages = (
        ((c_aq, c_ak), lambda r: emit_views(head_norm(r, gq_ref), aq_ref, aq4_ref, aq16_ref, rel.at[0], rel4.at[0])),
        ((c_v, c_o), store(v_ref)),
        ((c_ak, c_av), lambda r: emit_views(head_norm(r, gk_ref), ak_ref, ak4_ref, ak16_ref, rel.at[1], rel4.at[1])),
        ((c_o, c_z), store(o_ref)),
        ((c_av, c_az), lambda r: emit_views(r, av_ref, av4_ref, av16_ref, rel.at[2], rel4.at[2])),
        ((c_z, c_aq), store(z_ref)),
        ((c_qk, c_v), conv_silu),
        ((c_az, c_gate), store(az_ref)),
        ((c_gate, c_gate + GATE_PAD), gate_logs),
    )
    for cols, epilogue in stages:
        epilogue(proj(*cols))


def _inproj(x2d, norm_g, w_all, bd, gq, gk, conv_w, conv_b, gate_b, tril, S):
    T = x2d.shape[0]
    tm = TM_PROJ
    assert tm % tril.shape[0] == 0
    const = lambda i: (0, 0)
    row = lambda i: (i, 0)
    chunk = tril.shape[0]
    widths = (M_QK_WIDTH, M_WIDTH, M_WIDTH, M_WIDTH, A_WIDTH)
    out_shape = [jax.ShapeDtypeStruct((T, w), BF16) for w in widths] + [jax.ShapeDtypeStruct((T, GATE_PAD), F32)]
    out_specs = [pl.BlockSpec((tm, w), row) for w in widths] + [pl.BlockSpec((tm, GATE_PAD), row)]
    out_shape += [jax.ShapeDtypeStruct((T // chunk, M_HEADS, M_QK_DIM, chunk), BF16)]
    out_specs += [pl.BlockSpec((tm // chunk, M_HEADS, M_QK_DIM, chunk), lambda i: (i, 0, 0, 0))]
    for _, d in DILATED_PATTERNS:
        out_shape += [jax.ShapeDtypeStruct((T // d, d * A_WIDTH), BF16)] * 3
        out_specs += [pl.BlockSpec((tm // d, d * A_WIDTH), row)] * 3
    return pl.pallas_call(
        functools.partial(_inproj_kernel, S // tm),
        grid=(T // tm,),
        in_specs=[
            pl.BlockSpec((tm, D_MODEL), row),
            pl.BlockSpec((1, D_MODEL), const),
            pl.BlockSpec((D_MODEL, N_MAIN + GATE_PAD), const, pipeline_mode=pl.Buffered(1)),
            pl.BlockSpec(bd.shape, const),
            pl.BlockSpec((1, A_WIDTH), const),
            pl.BlockSpec((1, A_WIDTH), const),
            pl.BlockSpec((CONV_K, 2 * M_QK_WIDTH), const),
            pl.BlockSpec((1, 2 * M_QK_WIDTH), const),
            pl.BlockSpec((1, GATE_PAD), const),
            pl.BlockSpec(tril.shape, const),
        ],
        out_specs=out_specs,
        out_shape=out_shape,
        scratch_shapes=[
            pltpu.VMEM((3, A_WIDTH // LANES, tm, LANES), F32),
            pltpu.VMEM((3, 4 * A_WIDTH // LANES, tm // 4, LANES), F32),
            pltpu.VMEM((tm + 8, 2 * M_QK_WIDTH), F32),
            pltpu.VMEM((tm, D_MODEL), BF16),
        ],
        compiler_params=pltpu.CompilerParams(dimension_semantics=("arbitrary",), vmem_limit_bytes=VMEM_LIMIT),
        name="inproj",
    )(x2d, norm_g, w_all, bd, gq, gk, conv_w, conv_b, gate_b, tril)


def _mlstm_kernel(q_ref, kt_ref, v_ref, o_ref, z_ref, gate_ref, mg_ref, out_ref, c_ref, n_ref, m_ref, cbias_ref):
    nseq, L = q_ref.shape[0], q_ref.shape[1]

    @pl.when(pl.program_id(1) == 0)
    def _():
        c_ref[...] = jnp.zeros(c_ref.shape, F32)
        n_ref[...] = jnp.zeros(n_ref.shape, F32)
        m_ref[...] = jnp.zeros(m_ref.shape, F32)

    t_idx = lax.broadcasted_iota(jnp.int32, (L, L), 0)
    s_idx = lax.broadcasted_iota(jnp.int32, (L, L), 1)
    cbias_ref[...] = jnp.where(s_idx <= t_idx, 0.0, NEG_BIG)
    lane = lax.broadcasted_iota(jnp.int32, (L, LANES), 1)
    ones = jnp.ones((L, LANES), BF16)

    heads = range(M_HEADS)

    def latency_stage(i):
        unit = [i * M_HEADS + h for h in heads]
        q = [q_ref[i, :, h * M_QK_DIM:(h + 1) * M_QK_DIM] for h in heads]
        kt = [kt_ref[i, h] for h in heads]
        v = [v_ref[i, :, h * M_V_DIM:(h + 1) * M_V_DIM] for h in heads]
        m_prev = [m_ref[u] for u in unit]
        c_prev = [c_ref[u] for u in unit]
        n_prev = [n_ref[u] for u in unit]
        qk = [_dot(q[h], kt[h]) for h in heads]
        qc = [_dot(q[h], jnp.concatenate([c_prev[h], n_prev[h]], axis=1).astype(BF16)) for h in heads]

        gb = gate_ref[i]
        gbt = gb.T[0:2 * M_HEADS, :]
        b_row = [gbt[M_HEADS + h:M_HEADS + h + 1, :] for h in heads]
        a_row = [gbt[h:h + 1, :] - b_row[h] for h in heads]

        for h in heads:
            b_last = b_row[h][:, L - 1:L]
            log_state = b_last + a_row[h]
            m_new = jnp.maximum(b_last + m_prev[h], jnp.max(log_state, axis=1, keepdims=True))
            decay = jnp.exp2(b_last + m_prev[h] - m_new)
            kw = kt[h].astype(F32) * jnp.exp2(log_state - m_new)
            c_ref[unit[h]] = decay * c_prev[h] + _dot(kw.astype(BF16), v[h])
            n_ref[unit[h]] = decay * n_prev[h] + jnp.sum(kw, axis=1, keepdims=True)
            m_ref[unit[h]] = m_new

        rep = lambda col: jnp.broadcast_to(col, (L, LANES))
        c_t, floor, w_inter = [], [], []
        for h in heads:
            b_col = rep(jnp.sum(jnp.where(lane == M_HEADS + h, gb, 0.0), axis=1, keepdims=True))
            c_t.append(jnp.maximum(rep(m_prev[h]), rep(jnp.max(a_row[h] + cbias_ref[...], axis=1, keepdims=True))))
            w_inter.append(jnp.exp2(rep(m_prev[h]) - c_t[h]))
            floor.append(jnp.exp2(-(b_col + c_t[h])))
        return v, qk, qc, a_row, c_t, w_inter, floor

    def throughput_stage(i, v, qk, qc, a_row, c_t, w_inter, floor):
        wide = lambda t, width: jnp.concatenate([t] * (width // LANES), axis=1)
        s = [(qk[h] * jnp.exp2((a_row[h] + cbias_ref[...]) - wide(c_t[h], L))).astype(BF16) for h in heads]
        sv = [_dot(s[h], jnp.concatenate([v[h], ones], axis=1)) for h in heads]

        for h in heads:
            num = sv[h][:, :M_V_DIM] + wide(w_inter[h], M_V_DIM) * qc[h][:, :M_V_DIM]
            den = sv[h][:, M_V_DIM:] + w_inter[h] * qc[h][:, M_V_DIM:]
            den = jnp.maximum(jnp.abs(den), floor[h])
            hid = num / jnp.concatenate([den] * (M_V_DIM // LANES), axis=1)
            sl = slice(h * M_V_DIM, (h + 1) * M_V_DIM)
            hg = _sigmoid(o_ref[i, :, sl].astype(F32)) * hid
            hn = hg * lax.rsqrt(jnp.mean(hg * hg, axis=-1, keepdims=True) + EPS) * mg_ref[:, sl]
            zz = z_ref[i, :, sl].astype(F32)
            out_ref[i, :, sl] = (hn * (zz * _sigmoid(zz))).astype(BF16)

    staged = [latency_stage(i) for i in range(nseq)]
    for i in range(nseq):
        throughput_stage(i, *staged[i])


def _mlstm(q, kt, v, o, z, gates, m_norm_g, B, S):
    L = M_CHUNK
    nseq = M_SEQS_PER_STEP
    const = lambda g, s: (0, 0)
    seqs = lambda g, s: (g, 0, s, 0)
    q, v, o, z, gates = [a.reshape(B // nseq, nseq, S, a.shape[-1]) for a in (q, v, o, z, gates)]
    kt = kt.reshape(B // nseq, nseq, S // L, M_HEADS, M_QK_DIM, L)
    operands = [q, kt, v, o, z, gates]
    out = pl.pallas_call(
        _mlstm_kernel,
        grid=(B // nseq, S // L),
        in_specs=[pl.BlockSpec((None, nseq, L, a.shape[-1]), seqs) if a.ndim == 4 else
                  pl.BlockSpec((None, nseq, None, M_HEADS, M_QK_DIM, L), lambda g, s: (g, 0, s, 0, 0, 0))
                  for a in operands]
        + [pl.BlockSpec((1, M_WIDTH), const)],
        out_specs=pl.BlockSpec((None, nseq, L, M_WIDTH), seqs),
        out_shape=jax.ShapeDtypeStruct((B // nseq, nseq, S, M_WIDTH), BF16),
        scratch_shapes=[
            pltpu.VMEM((nseq * M_HEADS, M_QK_DIM, M_V_DIM), F32),
            pltpu.VMEM((nseq * M_HEADS, M_QK_DIM, LANES), F32),
            pltpu.VMEM((nseq * M_HEADS, 1, 1), F32),
            pltpu.VMEM((L, L), F32),
        ],
        compiler_params=pltpu.CompilerParams(dimension_semantics=("arbitrary", "arbitrary"),
                                             vmem_limit_bytes=VMEM_LIMIT),
        name="mlstm",
    )(*operands, m_norm_g)
    return out.reshape(B * S, M_WIDTH)


def _attn_kernel(q_ref, kp_ref, kc_ref, vp_ref, vc_ref, o_ref, lse_ref, bias_ref, bias0_ref):
    TQ = q_ref.shape[0]
    i = pl.program_id(2)
    qi = lax.broadcasted_iota(jnp.int32, (2 * A_BLOCK, 2 * A_BLOCK), 0) % A_BLOCK
    kc = lax.broadcasted_iota(jnp.int32, (2 * A_BLOCK, 2 * A_BLOCK), 1)
    band = (kc >= qi) & (kc <= qi + A_BLOCK)
    bias_ref[...] = jnp.where(band, 0.0, NEG_BIG)
    bias0_ref[...] = jnp.where(band & (kc >= jnp.where(i > 0, 0, A_BLOCK)), 0.0, NEG_BIG)
    lane = lax.broadcasted_iota(jnp.int32, (A_BLOCK, LANES), 1)
    low_half = lane < A_HEAD_DIM
    zero = jnp.zeros((A_BLOCK, LANES), BF16)
    ones = jnp.ones((2 * A_BLOCK, LANES), BF16)

    n_res = q_ref.shape[1] // A_WIDTH
    for res, j in [(res, j) for res in range(n_res) for j in range(TQ // A_BLOCK)]:
        bias = bias0_ref if j == 0 else bias_ref
        rows = slice(j * A_BLOCK, (j + 1) * A_BLOCK)
        mx_tile = jnp.zeros((A_BLOCK, LANES), F32)
        den_tile = jnp.ones((A_BLOCK, LANES), F32)
        for p in range(A_HEADS // 2):
            cols = slice(res * A_WIDTH + p * LANES, res * A_WIDTH + (p + 1) * LANES)
            if j == 0:
                kk = jnp.concatenate([kp_ref[:, cols], kc_ref[0:A_BLOCK, cols]], axis=0)
                vv = jnp.concatenate([vp_ref[:, cols], vc_ref[0:A_BLOCK, cols]], axis=0)
            else:
                kk = kc_ref[(j - 1) * A_BLOCK:(j + 1) * A_BLOCK, cols]
                vv = vc_ref[(j - 1) * A_BLOCK:(j + 1) * A_BLOCK, cols]
            q2 = q_ref[rows, cols]
            qs = jnp.concatenate([jnp.where(low_half, q2, zero), jnp.where(low_half, zero, q2)], axis=0)
            s = lax.dot_general(qs, kk, (((1,), (1,)), ((), ())), preferred_element_type=F32)
            s = s + bias[...]
            mx = jnp.max(s, axis=1, keepdims=True)
            pr = jnp.exp2(s - mx).astype(BF16)
            pv = _dot(pr, jnp.concatenate([vv, ones], axis=1))
            den = pv[:, LANES:]
            out = pv[:, :LANES] / den
            o_ref[rows, cols] = jnp.where(low_half, out[0:A_BLOCK], out[A_BLOCK:]).astype(BF16)
            mx_tile = jnp.where(lane == 2 * p, mx[0:A_BLOCK], jnp.where(lane == 2 * p + 1, mx[A_BLOCK:], mx_tile))
            den_tile = jnp.where(lane == 2 * p, den[0:A_BLOCK], jnp.where(lane == 2 * p + 1, den[A_BLOCK:], den_tile))
        lse_ref[rows, res * LANES:(res + 1) * LANES] = mx_tile * LN2 + jnp.log(den_tile)


def _attn_pattern(qv, kv, vv, B, S, dilation):
    d = dilation
    L = S // d
    TQ = min(A_ROWS_PER_STEP, L)
    nq = L // TQ
    r = TQ // A_BLOCK
    n_res = min(d, A_ROWS_PER_STEP // TQ)
    cur = lambda b, rr, i: (b * nq + i, rr)
    prev = lambda b, rr, i: (b * (L // A_BLOCK) + jnp.maximum(i * r - 1, 0), rr)
    return pl.pallas_call(
        _attn_kernel,
        grid=(B, d // n_res, nq),
        in_specs=[
            pl.BlockSpec((TQ, n_res * A_WIDTH), cur),
            pl.BlockSpec((A_BLOCK, n_res * A_WIDTH), prev),
            pl.BlockSpec((TQ, n_res * A_WIDTH), cur),
            pl.BlockSpec((A_BLOCK, n_res * A_WIDTH), prev),
            pl.BlockSpec((TQ, n_res * A_WIDTH), cur),
        ],
        out_specs=[
            pl.BlockSpec((TQ, n_res * A_WIDTH), cur),
            pl.BlockSpec((TQ, n_res * LANES), cur),
        ],
        out_shape=[
            jax.ShapeDtypeStruct((B * L, d * A_WIDTH), BF16),
            jax.ShapeDtypeStruct((B * L, d * LANES), F32),
        ],
        scratch_shapes=[
            pltpu.VMEM((2 * A_BLOCK, 2 * A_BLOCK), F32),
            pltpu.VMEM((2 * A_BLOCK, 2 * A_BLOCK), F32),
        ],
        compiler_params=pltpu.CompilerParams(dimension_semantics=("arbitrary", "arbitrary", "arbitrary"),
                                             vmem_limit_bytes=VMEM_LIMIT),
        name=f"attn_d{d}",
    )(qv, kv, kv, vv, vv)


def _outproj_kernel(x_ref, hm_ref, o1_ref, o4_ref, o16_ref, l1_ref, l4_ref, l16_ref, az_ref, w_ref, ex_ref,
                    out_ref, o4n, o16n, l4n, l16n, ha_scr):
    tm = x_ref.shape[0]
    out_ref[...] = x_ref[...] + _dot(hm_ref[...], w_ref[0:M_WIDTH, :])
    for d, o_ref, l_ref, o_nat, l_nat in ((4, o4_ref, l4_ref, o4n, l4n), (16, o16_ref, l16_ref, o16n, l16n)):
        for r in range(d):
            l_nat[pl.ds(r, tm // d, stride=d), :] = l_ref[:, r * LANES:(r + 1) * LANES]
            for sl in range(A_WIDTH // LANES):
                c0 = r * A_WIDTH + sl * LANES
                o_nat[sl, pl.ds(r, tm // d, stride=d), :] = o_ref[:, c0:c0 + LANES].astype(F32)

    l1, l2, l3 = l1_ref[...], l4n[...], l16n[...]
    mx = jnp.maximum(jnp.maximum(l1, l2), l3)
    e1, e2, e3 = jnp.exp(l1 - mx), jnp.exp(l2 - mx), jnp.exp(l3 - mx)
    tot = e1 + e2 + e3
    lane = lax.broadcasted_iota(jnp.int32, (tm, LANES), 1)
    packed = None
    for n, w in enumerate((e1 / tot, e2 / tot, e3 / tot)):
        hi = w.astype(BF16).astype(F32)
        for part in (hi, w - hi):
            slot = 2 * n + (0 if part is hi else 1)
            packed = part if packed is None else jnp.where(lane < A_HEADS * slot, packed,
                                                             pltpu.roll(part, A_HEADS * slot, axis=1))
    wexp = _dot(packed.astype(BF16), ex_ref[...])
    for sl in range(A_WIDTH // LANES):
        cols = slice(sl * LANES, (sl + 1) * LANES)
        pats = (o1_ref[:, cols].astype(F32), o4n[sl], o16n[sl])
        wcol = lambda n: wexp[:, n * A_WIDTH + sl * LANES:n * A_WIDTH + (sl + 1) * LANES]
        acc = wcol(0) * pats[0] + wcol(1) * pats[1] + wcol(2) * pats[2]
        zz = az_ref[:, cols].astype(F32)
        ha_scr[:, cols] = (acc * (zz * _sigmoid(zz))).astype(BF16)

    out_ref[...] += _dot(ha_scr[...], w_ref[M_WIDTH:D_MIX, :])


def _outproj(x2d, hm, outs, lses, az, w_out, expand):
    T = x2d.shape[0]
    tm = TM_PROJ
    row = lambda i: (i, 0)
    const = lambda i: (0, 0)
    dils = [d for _, d in DILATED_PATTERNS]
    return pl.pallas_call(
        _outproj_kernel,
        grid=(T // tm,),
        in_specs=[pl.BlockSpec((tm, D_MODEL), row), pl.BlockSpec((tm, M_WIDTH), row)]
        + [pl.BlockSpec((tm // d, d * A_WIDTH), row) for d in dils]
        + [pl.BlockSpec((tm // d, d * LANES), row) for d in dils]
        + [pl.BlockSpec((tm, A_WIDTH), row), pl.BlockSpec((D_MIX, D_MODEL), const),
           pl.BlockSpec(expand.shape, const)],
        out_specs=pl.BlockSpec((tm, D_MODEL), row),
        out_shape=jax.ShapeDtypeStruct((T, D_MODEL), F32),
        scratch_shapes=[
            pltpu.VMEM((A_WIDTH // LANES, tm, LANES), F32),
            pltpu.VMEM((A_WIDTH // LANES, tm, LANES), F32),
            pltpu.VMEM((tm, LANES), F32),
            pltpu.VMEM((tm, LANES), F32),
            pltpu.VMEM((tm, A_WIDTH), BF16),
        ],
        compiler_params=pltpu.CompilerParams(dimension_semantics=("arbitrary",), vmem_limit_bytes=VMEM_LIMIT),
        name="outproj",
    )(x2d, hm, *outs, *lses, az, w_out, expand)


def _constants():
    idx = jnp.arange(MXU_TILE)
    bd = (idx[:, None] // A_HEAD_DIM == idx[None, :] // A_HEAD_DIM).astype(BF16)
    t = jnp.arange(M_CHUNK)
    tril = (t[None, :] <= t[:, None]).astype(BF16)
    row = jnp.arange(LANES)[:, None]
    col = jnp.arange(len(DILATED_PATTERNS) * A_WIDTH)[None, :]
    expand = ((row // (2 * A_HEADS) == col // A_WIDTH) & (row % A_HEADS == (col % A_WIDTH) // A_HEAD_DIM)
              & (row < 2 * A_HEADS * len(DILATED_PATTERNS))).astype(BF16)
    return bd, tril, expand


def kernel(x, norm_g, w_in, gate_b, conv_w, conv_b, m_norm_g, q_norm_g, k_norm_g, w_out):
    B, S, D = x.shape
    depth = norm_g.shape[0]
    assert D == D_MODEL and S % M_CHUNK == 0 and S % TM_PROJ == 0 and B % M_SEQS_PER_STEP == 0
    for window, dilation in DILATED_PATTERNS:
        assert window == dilation * A_BLOCK and S % (dilation * A_BLOCK) == 0 and TM_PROJ % (16 * dilation) == 0
    bd, tril, expand = _constants()
    n_gate0 = 2 * M_QK_WIDTH + 3 * M_WIDTH
    x2d = x.reshape(B * S, D)
    for l in range(depth):
        w = w_in[l]
        w_all = jnp.concatenate(
            [w[:, :n_gate0], w[:, n_gate0 + 2 * M_HEADS:],
             jnp.pad(w[:, n_gate0:n_gate0 + 2 * M_HEADS], ((0, 0), (0, GATE_PAD - 2 * M_HEADS)))],
            axis=1).astype(BF16)
        gq = (jnp.tile(q_norm_g[l], A_HEADS) * (A_HEAD_DIM ** -0.5 * LOG2E))[None, :]
        gk = jnp.tile(k_norm_g[l], A_HEADS)[None, :]
        gb = jnp.pad(gate_b[l], (0, GATE_PAD - 2 * M_HEADS))[None, :]
        mq, mv, mo, mz, az, gates, mkt, *qkv_views = _inproj(x2d, norm_g[l][None, :], w_all, bd, gq, gk,
                                                             conv_w[l], conv_b[l][None, :], gb, tril, S)
        hm = _mlstm(mq, mkt, mv, mo, mz, gates, m_norm_g[l][None, :], B, S)
        outs, lses = [], []
        for n, (_, dilation) in enumerate(DILATED_PATTERNS):
            o, lse = _attn_pattern(*qkv_views[3 * n:3 * n + 3], B, S, dilation)
            outs.append(o)
            lses.append(lse)
        x2d = _outproj(x2d, hm, outs, lses, az, w_out[l].astype(BF16), expand)
    return x2d.reshape(B, S, D)
```

```python
import functools

import jax
import jax.numpy as jnp
from jax import lax
from jax.experimental import pallas as pl
from jax.experimental.pallas import tpu as pltpu

F32 = jnp.float32
BF16 = jnp.bfloat16

EPS = 1e-6
D_MODEL = 1024
M_HEADS = 4
M_V_DIM = 256
M_QK_DIM = 128
M_QK_WIDTH = M_HEADS * M_QK_DIM
M_WIDTH = M_HEADS * M_V_DIM
CONV_K = 4
A_HEAD_DIM = 64
A_HEADS = 8
A_WIDTH = A_HEADS * A_HEAD_DIM
A_BLOCK = 128
DILATED_PATTERNS = ((128, 1), (512, 4), (2048, 16))
D_MIX = M_WIDTH + A_WIDTH

LANES = 128
MXU_TILE = 256
GATE_PAD = LANES
N_MAIN = 2 * M_QK_WIDTH + 3 * M_WIDTH + 4 * A_WIDTH
NEG_BIG = -1e30
LOG2E = 1.4426950408889634
LN2 = 0.6931471805599453

V7X_VMEM_BYTES = 64 * 1024 * 1024
VMEM_LIMIT = V7X_VMEM_BYTES * 7 // 8

TM_PROJ = 512
TM_OUT = 512
A_ROWS_PER_STEP = 1024
M_CHUNK = 256
M_SEQS_PER_STEP = 2


def _sigmoid(x):
    return 1.0 / (1.0 + jnp.exp2(x * (-LOG2E)))


def _split3(x):
    hi = x.astype(BF16)
    r1 = x - hi.astype(F32)
    mid = r1.astype(BF16)
    lo = (r1 - mid.astype(F32)).astype(BF16)
    return hi, mid, lo


def _split2(x):
    hi = x.astype(BF16)
    return hi, (x - hi.astype(F32)).astype(BF16)


def _dot(a, b):
    return jnp.dot(a, b, preferred_element_type=F32)


def _inproj_kernel(tiles_per_seq, x_ref, g_ref, w_ref, bd_ref, gq_ref, gk_ref, cw_ref, cb_ref, gb_ref, tril_ref,
                   q_ref, v_ref, o_ref, z_ref, az_ref, gate_ref, kt_ref,
                   aq_ref, ak_ref, av_ref, aq4_ref, ak4_ref, av4_ref, aq16_ref, ak16_ref, av16_ref, rel, rel4, cbuf, h_ref):
    tm = x_ref.shape[0]
    first = (pl.program_id(0) % tiles_per_seq) == 0

    @pl.when(first)
    def _():
        cbuf[0:8, :] = jnp.zeros((8, cbuf.shape[1]), F32)

    @pl.when(jnp.logical_not(first))
    def _():
        cbuf[0:8, :] = cbuf[tm:tm + 8, :]

    x = x_ref[...]
    h_ref[...] = (x * g_ref[...]).astype(BF16)
    rs = lax.rsqrt(jnp.mean(x * x, axis=-1, keepdims=True) + EPS)

    def proj(c0, c1):
        return _dot(h_ref[...], w_ref[:, c0:c1]) * rs

    def head_norm(a, gain_ref):
        sq = (a * a).astype(BF16)
        bd = bd_ref[...]
        half = bd.shape[0]
        ss = jnp.concatenate([_dot(sq[:, s:s + half], bd) for s in range(0, A_WIDTH, half)], axis=1)
        return a * lax.rsqrt(ss * (1.0 / A_HEAD_DIM) + EPS) * gain_ref[...]

    def emit_views(val, nat_ref, v4_ref, v16_ref, rel, rel4):
        nsl = A_WIDTH // LANES
        nat_ref[...] = val.astype(BF16)
        for sl in range(nsl):
            rel[sl] = val[:, sl * LANES:(sl + 1) * LANES]
        for r4 in range(4):
            for sl in range(nsl):
                blk = rel[sl, pl.ds(r4, tm // 4, stride=4), :]
                rel4[r4 * nsl + sl] = blk
                v4_ref[:, r4 * A_WIDTH + sl * LANES:r4 * A_WIDTH + (sl + 1) * LANES] = blk.astype(BF16)
        for r in range(16):
            for sl in range(nsl):
                c0 = r * A_WIDTH + sl * LANES
                v16_ref[:, c0:c0 + LANES] = rel4[(r % 4) * nsl + sl, pl.ds(r // 4, tm // 16, stride=4), :].astype(BF16)

    c_qk, c_v, c_o, c_z = 0, 2 * M_QK_WIDTH, 2 * M_QK_WIDTH + M_WIDTH, 2 * M_QK_WIDTH + 2 * M_WIDTH
    c_aq = c_z + M_WIDTH
    c_ak, c_av, c_az, c_gate = c_aq + A_WIDTH, c_aq + 2 * A_WIDTH, c_aq + 3 * A_WIDTH, c_aq + 4 * A_WIDTH

    def conv_silu(raw):
        cbuf[8:8 + tm, :] = raw
        conv = cb_ref[...] + cw_ref[0:1, :] * cbuf[5:5 + tm, :]
        for j in range(1, CONV_K):
            conv = conv + cw_ref[j:j + 1, :] * cbuf[5 + j:5 + j + tm, :]
        act = conv * _sigmoid(conv)
        q_ref[...] = (act[:, 0:M_QK_WIDTH] * (M_QK_DIM ** -0.5)).astype(BF16)
        chunk = kt_ref.shape[-1]
        for c in range(tm // chunk):
            for hd in range(M_HEADS):
                c0 = M_QK_WIDTH + hd * M_QK_DIM
                kt_ref[c, hd] = act[c * chunk:(c + 1) * chunk, c0:c0 + M_QK_DIM].T.astype(BF16)

    def gate_logs(raw):
        g = raw + gb_ref[...]
        logf = (jnp.minimum(g, 0.0) - jnp.log1p(jnp.exp(-jnp.abs(g)))) * LOG2E
        hi, mid, lo = _split3(logf)
        tril = tril_ref[...]
        L = tril.shape[0]
        lane = lax.broadcasted_iota(jnp.int32, (L, LANES), 1)
        for c0 in range(0, tm, L):
            cum = _dot(tril, hi[c0:c0 + L]) + _dot(tril, mid[c0:c0 + L]) + _dot(tril, lo[c0:c0 + L])
            gate_ref[c0:c0 + L, :] = jnp.where(lane < M_HEADS, g[c0:c0 + L] * LOG2E, cum)

    def store(ref):
        def epilogue(raw):
            ref[...] = raw.astype(BF16)
        return epilogue

    stages = (
        ((c_v, c_o), store(v_ref)),
        ((c_qk, c_v), conv_silu),
        ((c_gate, c_gate + GATE_PAD), gate_logs),
        ((c_o, c_z), store(o_ref)),
        ((c_aq, c_ak), lambda r: emit_views(head_norm(r, gq_ref), aq_ref, aq4_ref, aq16_ref, rel.at[0], rel4.at[0])),
        ((c_z, c_aq), store(z_ref)),
        ((c_ak, c_av), lambda r: emit_views(head_norm(r, gk_ref), ak_ref, ak4_ref, ak16_ref, rel.at[1], rel4.at[1])),
        ((c_az, c_gate), store(az_ref)),
        ((c_av, c_az), lambda r: emit_views(r, av_ref, av4_ref, av16_ref, rel.at[2], rel4.at[2])),
    )
    for cols, epilogue in stages:
        epilogue(proj(*cols))


def _inproj(x2d, norm_g, w_all, bd, gq, gk, conv_w, conv_b, gate_b, tril, S):
    T = x2d.shape[0]
    tm = TM_PROJ
    assert tm % tril.shape[0] == 0
    const = lambda i: (0, 0)
    row = lambda i: (i, 0)
    chunk = tril.shape[0]
    widths = (M_QK_WIDTH, M_WIDTH, M_WIDTH, M_WIDTH, A_WIDTH)
    out_shape = [jax.ShapeDtypeStruct((T, w), BF16) for w in widths] + [jax.ShapeDtypeStruct((T, GATE_PAD), F32)]
    out_specs = [pl.BlockSpec((tm, w), row) for w in widths] + [pl.BlockSpec((tm, GATE_PAD), row)]
    out_shape += [jax.ShapeDtypeStruct((T // chunk, M_HEADS, M_QK_DIM, chunk), BF16)]
    out_specs += [pl.BlockSpec((tm // chunk, M_HEADS, M_QK_DIM, chunk), lambda i: (i, 0, 0, 0))]
    for _, d in DILATED_PATTERNS:
        out_shape += [jax.ShapeDtypeStruct((T // d, d * A_WIDTH), BF16)] * 3
        out_specs += [pl.BlockSpec((tm // d, d * A_WIDTH), row)] * 3
    return pl.pallas_call(
        functools.partial(_inproj_kernel, S // tm),
        grid=(T // tm,),
        in_specs=[
            pl.BlockSpec((tm, D_MODEL), row),
            pl.BlockSpec((1, D_MODEL), const),
            pl.BlockSpec((D_MODEL, N_MAIN + GATE_PAD), const, pipeline_mode=pl.Buffered(1)),
            pl.BlockSpec(bd.shape, const),
            pl.BlockSpec((1, A_WIDTH), const),
            pl.BlockSpec((1, A_WIDTH), const),
            pl.BlockSpec((CONV_K, 2 * M_QK_WIDTH), const),
            pl.BlockSpec((1, 2 * M_QK_WIDTH), const),
            pl.BlockSpec((1, GATE_PAD), const),
            pl.BlockSpec(tril.shape, const),
        ],
        out_specs=out_specs,
        out_shape=out_shape,
        scratch_shapes=[
            pltpu.VMEM((3, A_WIDTH // LANES, tm, LANES), F32),
            pltpu.VMEM((3, 4 * A_WIDTH // LANES, tm // 4, LANES), F32),
            pltpu.VMEM((tm + 8, 2 * M_QK_WIDTH), F32),
            pltpu.VMEM((tm, D_MODEL), BF16),
        ],
        compiler_params=pltpu.CompilerParams(dimension_semantics=("arbitrary",), vmem_limit_bytes=VMEM_LIMIT),
        name="inproj",
    )(x2d, norm_g, w_all, bd, gq, gk, conv_w, conv_b, gate_b, tril)


def _mlstm_kernel(q_ref, kt_ref, v_ref, o_ref, z_ref, gate_ref, mg_ref, out_ref, c_ref, n_ref, m_ref, cbias_ref):
    nseq, L = q_ref.shape[0], q_ref.shape[1]

    @pl.when(pl.program_id(1) == 0)
    def _():
        c_ref[...] = jnp.zeros(c_ref.shape, F32)
        n_ref[...] = jnp.zeros(n_ref.shape, F32)
        m_ref[...] = jnp.zeros(m_ref.shape, F32)

    t_idx = lax.broadcasted_iota(jnp.int32, (L, L), 0)
    s_idx = lax.broadcasted_iota(jnp.int32, (L, L), 1)
    cbias_ref[...] = jnp.where(s_idx <= t_idx, 0.0, NEG_BIG)
    lane = lax.broadcasted_iota(jnp.int32, (L, LANES), 1)
    ones = jnp.ones((L, LANES), BF16)

    heads = range(M_HEADS)

    def latency_stage(i):
        unit = [i * M_HEADS + h for h in heads]
        q = [q_ref[i, :, h * M_QK_DIM:(h + 1) * M_QK_DIM] for h in heads]
        kt = [kt_ref[i, h] for h in heads]
        v = [v_ref[i, :, h * M_V_DIM:(h + 1) * M_V_DIM] for h in heads]
        m_prev = [m_ref[u] for u in unit]
        c_prev = [c_ref[u] for u in unit]
        n_prev = [n_ref[u] for u in unit]
        qk = [_dot(q[h], kt[h]) for h in heads]
        qc = [_dot(q[h], jnp.concatenate([c_prev[h], n_prev[h]], axis=1).astype(BF16)) for h in heads]

        gb = gate_ref[i]
        gbt = gb.T[0:2 * M_HEADS, :]
        b_row = [gbt[M_HEADS + h:M_HEADS + h + 1, :] for h in heads]
        a_row = [gbt[h:h + 1, :] - b_row[h] for h in heads]

        for h in heads:
            b_last = b_row[h][:, L - 1:L]
            log_state = b_last + a_row[h]
            m_new = jnp.maximum(b_last + m_prev[h], jnp.max(log_state, axis=1, keepdims=True))
            decay = jnp.exp2(b_last + m_prev[h] - m_new)
            kw = kt[h].astype(F32) * jnp.exp2(log_state - m_new)
            c_ref[unit[h]] = decay * c_prev[h] + _dot(kw.astype(BF16), v[h])
            n_ref[unit[h]] = decay * n_prev[h] + jnp.sum(kw, axis=1, keepdims=True)
            m_ref[unit[h]] = m_new

        rep = lambda col: jnp.broadcast_to(col, (L, LANES))
        c_t, floor, w_inter = [], [], []
        for h in heads:
            b_col = rep(jnp.sum(jnp.where(lane == M_HEADS + h, gb, 0.0), axis=1, keepdims=True))
            c_t.append(jnp.maximum(rep(m_prev[h]), rep(jnp.max(a_row[h] + cbias_ref[...], axis=1, keepdims=True))))
            w_inter.append(jnp.exp2(rep(m_prev[h]) - c_t[h]))
            floor.append(jnp.exp2(-(b_col + c_t[h])))
        return v, qk, qc, a_row, c_t, w_inter, floor

    def throughput_stage(i, v, qk, qc, a_row, c_t, w_inter, floor):
        wide = lambda t, width: jnp.concatenate([t] * (width // LANES), axis=1)
        s = [(qk[h] * jnp.exp2((a_row[h] + cbias_ref[...]) - wide(c_t[h], L))).astype(BF16) for h in heads]
        sv = [_dot(s[h], jnp.concatenate([v[h], ones], axis=1)) for h in heads]

        for h in heads:
            num = sv[h][:, :M_V_DIM] + wide(w_inter[h], M_V_DIM) * qc[h][:, :M_V_DIM]
            den = sv[h][:, M_V_DIM:] + w_inter[h] * qc[h][:, M_V_DIM:]
            den = jnp.maximum(jnp.abs(den), floor[h])
            hid = num / jnp.concatenate([den] * (M_V_DIM // LANES), axis=1)
            sl = slice(h * M_V_DIM, (h + 1) * M_V_DIM)
            hg = _sigmoid(o_ref[i, :, sl].astype(F32)) * hid
            hn = hg * lax.rsqrt(jnp.mean(hg * hg, axis=-1, keepdims=True) + EPS) * mg_ref[:, sl]
            zz = z_ref[i, :, sl].astype(F32)
            out_ref[i, :, sl] = (hn * (zz * _sigmoid(zz))).astype(BF16)

    staged = [latency_stage(i) for i in range(nseq)]
    for i in range(nseq):
        throughput_stage(i, *staged[i])


def _mlstm(q, kt, v, o, z, gates, m_norm_g, B, S):
    L = M_CHUNK
    nseq = M_SEQS_PER_STEP
    const = lambda g, s: (0, 0)
    seqs = lambda g, s: (g, 0, s, 0)
    q, v, o, z, gates = [a.reshape(B // nseq, nseq, S, a.shape[-1]) for a in (q, v, o, z, gates)]
    kt = kt.reshape(B // nseq, nseq, S // L, M_HEADS, M_QK_DIM, L)
    operands = [q, kt, v, o, z, gates]
    out = pl.pallas_call(
        _mlstm_kernel,
        grid=(B // nseq, S // L),
        in_specs=[pl.BlockSpec((None, nseq, L, a.shape[-1]), seqs) if a.ndim == 4 else
                  pl.BlockSpec((None, nseq, None, M_HEADS, M_QK_DIM, L), lambda g, s: (g, 0, s, 0, 0, 0))
                  for a in operands]
        + [pl.BlockSpec((1, M_WIDTH), const)],
        out_specs=pl.BlockSpec((None, nseq, L, M_WIDTH), seqs),
        out_shape=jax.ShapeDtypeStruct((B // nseq, nseq, S, M_WIDTH), BF16),
        scratch_shapes=[
            pltpu.VMEM((nseq * M_HEADS, M_QK_DIM, M_V_DIM), F32),
            pltpu.VMEM((nseq * M_HEADS, M_QK_DIM, LANES), F32),
            pltpu.VMEM((nseq * M_HEADS, 1, 1), F32),
            pltpu.VMEM((L, L), F32),
        ],
        compiler_params=pltpu.CompilerParams(dimension_semantics=("arbitrary", "arbitrary"),
                                             vmem_limit_bytes=VMEM_LIMIT),
        name="mlstm",
    )(*operands, m_norm_g)
    return out.reshape(B * S, M_WIDTH)


def _attn_kernel(q_ref, kp_ref, kc_ref, vp_ref, vc_ref, o_ref, lse_ref, bias_ref, bias0_ref):
    TQ = q_ref.shape[0]
    i = pl.program_id(2)
    qi = lax.broadcasted_iota(jnp.int32, (2 * A_BLOCK, 2 * A_BLOCK), 0) % A_BLOCK
    kc = lax.broadcasted_iota(jnp.int32, (2 * A_BLOCK, 2 * A_BLOCK), 1)
    band = (kc >= qi) & (kc <= qi + A_BLOCK)
    bias_ref[...] = jnp.where(band, 0.0, NEG_BIG)
    bias0_ref[...] = jnp.where(band & (kc >= jnp.where(i > 0, 0, A_BLOCK)), 0.0, NEG_BIG)
    lane = lax.broadcasted_iota(jnp.int32, (A_BLOCK, LANES), 1)
    low_half = lane < A_HEAD_DIM
    zero = jnp.zeros((A_BLOCK, LANES), BF16)
    ones = jnp.ones((2 * A_BLOCK, LANES), BF16)

    n_res = q_ref.shape[1] // A_WIDTH
    for res, j in [(res, j) for res in range(n_res) for j in range(TQ // A_BLOCK)]:
        bias = bias0_ref if j == 0 else bias_ref
        rows = slice(j * A_BLOCK, (j + 1) * A_BLOCK)
        mx_tile = jnp.zeros((A_BLOCK, LANES), F32)
        den_tile = jnp.ones((A_BLOCK, LANES), F32)
        for p in range(A_HEADS // 2):
            cols = slice(res * A_WIDTH + p * LANES, res * A_WIDTH + (p + 1) * LANES)
            if j == 0:
                kk = jnp.concatenate([kp_ref[:, cols], kc_ref[0:A_BLOCK, cols]], axis=0)
                vv = jnp.concatenate([vp_ref[:, cols], vc_ref[0:A_BLOCK, cols]], axis=0)
            else:
                kk = kc_ref[(j - 1) * A_BLOCK:(j + 1) * A_BLOCK, cols]
                vv = vc_ref[(j - 1) * A_BLOCK:(j + 1) * A_BLOCK, cols]
            q2 = q_ref[rows, cols]
            qs = jnp.concatenate([jnp.where(low_half, q2, zero), jnp.where(low_half, zero, q2)], axis=0)
            s = lax.dot_general(qs, kk, (((1,), (1,)), ((), ())), preferred_element_type=F32)
            s = s + bias[...]
            mx = jnp.max(s, axis=1, keepdims=True)
            pr = jnp.exp2(s - mx).astype(BF16)
            pv = _dot(pr, jnp.concatenate([vv, ones], axis=1))
            den = pv[:, LANES:]
            out = pv[:, :LANES] / den
            o_ref[rows, cols] = jnp.where(low_half, out[0:A_BLOCK], out[A_BLOCK:]).astype(BF16)
            mx_tile = jnp.where(lane == 2 * p, mx[0:A_BLOCK], jnp.where(lane == 2 * p + 1, mx[A_BLOCK:], mx_tile))
            den_tile = jnp.where(lane == 2 * p, den[0:A_BLOCK], jnp.where(lane == 2 * p + 1, den[A_BLOCK:], den_tile))
        lse_ref[rows, res * LANES:(res + 1) * LANES] = mx_tile * LN2 + jnp.log(den_tile)


def _attn_pattern(qv, kv, vv, B, S, dilation):
    d = dilation
    L = S // d
    TQ = min(A_ROWS_PER_STEP, L)
    nq = L // TQ
    r = TQ // A_BLOCK
    n_res = min(d, A_ROWS_PER_STEP // TQ)
    cur = lambda b, rr, i: (b * nq + i, rr)
    prev = lambda b, rr, i: (b * (L // A_BLOCK) + jnp.maximum(i * r - 1, 0), rr)
    return pl.pallas_call(
        _attn_kernel,
        grid=(B, d // n_res, nq),
        in_specs=[
            pl.BlockSpec((TQ, n_res * A_WIDTH), cur),
            pl.BlockSpec((A_BLOCK, n_res * A_WIDTH), prev),
            pl.BlockSpec((TQ, n_res * A_WIDTH), cur),
            pl.BlockSpec((A_BLOCK, n_res * A_WIDTH), prev),
            pl.BlockSpec((TQ, n_res * A_WIDTH), cur),
        ],
        out_specs=[
            pl.BlockSpec((TQ, n_res * A_WIDTH), cur),
            pl.BlockSpec((TQ, n_res * LANES), cur),
        ],
        out_shape=[
            jax.ShapeDtypeStruct((B * L, d * A_WIDTH), BF16),
            jax.ShapeDtypeStruct((B * L, d * LANES), F32),
        ],
        scratch_shapes=[
            pltpu.VMEM((2 * A_BLOCK, 2 * A_BLOCK), F32),
            pltpu.VMEM((2 * A_BLOCK, 2 * A_BLOCK), F32),
        ],
        compiler_params=pltpu.CompilerParams(dimension_semantics=("arbitrary", "arbitrary", "arbitrary"),
                                             vmem_limit_bytes=VMEM_LIMIT),
        name=f"attn_d{d}",
    )(qv, kv, kv, vv, vv)


def _outproj_kernel(x_ref, hm_ref, o1_ref, o4_ref, o16_ref, l1_ref, l4_ref, l16_ref, az_ref, w_ref, ex_ref,
                    out_ref, o4n, o16n, l4n, l16n, ha_scr):
    tm = x_ref.shape[0]
    out_ref[...] = x_ref[...] + _dot(hm_ref[...], w_ref[0:M_WIDTH, :])
    for d, o_ref, l_ref, o_nat, l_nat in ((4, o4_ref, l4_ref, o4n, l4n), (16, o16_ref, l16_ref, o16n, l16n)):
        for r in range(d):
            l_nat[pl.ds(r, tm // d, stride=d), :] = l_ref[:, r * LANES:(r + 1) * LANES]
            for sl in range(A_WIDTH // LANES):
                c0 = r * A_WIDTH + sl * LANES
                o_nat[sl, pl.ds(r, tm // d, stride=d), :] = o_ref[:, c0:c0 + LANES].astype(F32)

    l1, l2, l3 = l1_ref[...], l4n[...], l16n[...]
    mx = jnp.maximum(jnp.maximum(l1, l2), l3)
    e1, e2, e3 = jnp.exp(l1 - mx), jnp.exp(l2 - mx), jnp.exp(l3 - mx)
    tot = e1 + e2 + e3
    lane = lax.broadcasted_iota(jnp.int32, (tm, LANES), 1)
    packed = None
    for n, w in enumerate((e1 / tot, e2 / tot, e3 / tot)):
        hi = w.astype(BF16).astype(F32)
        for part in (hi, w - hi):
            slot = 2 * n + (0 if part is hi else 1)
            packed = part if packed is None else jnp.where(lane < A_HEADS * slot, packed,
                                                             pltpu.roll(part, A_HEADS * slot, axis=1))
    wexp = _dot(packed.astype(BF16), ex_ref[...])
    for sl in range(A_WIDTH // LANES):
        cols = slice(sl * LANES, (sl + 1) * LANES)
        pats = (o1_ref[:, cols].astype(F32), o4n[sl], o16n[sl])
        wcol = lambda n: wexp[:, n * A_WIDTH + sl * LANES:n * A_WIDTH + (sl + 1) * LANES]
        acc = wcol(0) * pats[0] + wcol(1) * pats[1] + wcol(2) * pats[2]
        zz = az_ref[:, cols].astype(F32)
        ha_scr[:, cols] = (acc * (zz * _sigmoid(zz))).astype(BF16)

    out_ref[...] += _dot(ha_scr[...], w_ref[M_WIDTH:D_MIX, :])


def _outproj(x2d, hm, outs, lses, az, w_out, expand):
    T = x2d.shape[0]
    tm = TM_OUT
    row = lambda i: (i, 0)
    const = lambda i: (0, 0)
    dils = [d for _, d in DILATED_PATTERNS]
    return pl.pallas_call(
        _outproj_kernel,
        grid=(T // tm,),
        in_specs=[pl.BlockSpec((tm, D_MODEL), row), pl.BlockSpec((tm, M_WIDTH), row)]
        + [pl.BlockSpec((tm // d, d * A_WIDTH), row) for d in dils]
        + [pl.BlockSpec((tm // d, d * LANES), row) for d in dils]
        + [pl.BlockSpec((tm, A_WIDTH), row), pl.BlockSpec((D_MIX, D_MODEL), const),
           pl.BlockSpec(expand.shape, const)],
        out_specs=pl.BlockSpec((tm, D_MODEL), row),
        out_shape=jax.ShapeDtypeStruct((T, D_MODEL), F32),
        scratch_shapes=[
            pltpu.VMEM((A_WIDTH // LANES, tm, LANES), F32),
            pltpu.VMEM((A_WIDTH // LANES, tm, LANES), F32),
            pltpu.VMEM((tm, LANES), F32),
            pltpu.VMEM((tm, LANES), F32),
            pltpu.VMEM((tm, A_WIDTH), BF16),
        ],
        compiler_params=pltpu.CompilerParams(dimension_semantics=("arbitrary",), vmem_limit_bytes=VMEM_LIMIT),
        name="outproj",
    )(x2d, hm, *outs, *lses, az, w_out, expand)


def _constants():
    idx = jnp.arange(MXU_TILE)
    bd = (idx[:, None] // A_HEAD_DIM == idx[None, :] // A_HEAD_DIM).astype(BF16)
    t = jnp.arange(M_CHUNK)
    tril = (t[None, :] <= t[:, None]).astype(BF16)
    row = jnp.arange(LANES)[:, None]
    col = jnp.arange(len(DILATED_PATTERNS) * A_WIDTH)[None, :]
    expand = ((row // (2 * A_HEADS) == col // A_WIDTH) & (row % A_HEADS == (col % A_WIDTH) // A_HEAD_DIM)
              & (row < 2 * A_HEADS * len(DILATED_PATTERNS))).astype(BF16)
    return bd, tril, expand


def kernel(x, norm_g, w_in, gate_b, conv_w, conv_b, m_norm_g, q_norm_g, k_norm_g, w_out):
    B, S, D = x.shape
    depth = norm_g.shape[0]
    assert D == D_MODEL and S % M_CHUNK == 0 and S % TM_PROJ == 0 and B % M_SEQS_PER_STEP == 0
    assert (B * S) % TM_OUT == 0 and TM_OUT % (16 * 16) == 0
    assert tuple(d for _, d in DILATED_PATTERNS) == (1, 4, 16)
    for window, dilation in DILATED_PATTERNS:
        assert window == dilation * A_BLOCK and S % (dilation * A_BLOCK) == 0 and TM_PROJ % (16 * dilation) == 0
    bd, tril, expand = _constants()
    n_gate0 = 2 * M_QK_WIDTH + 3 * M_WIDTH
    x2d = x.reshape(B * S, D)
    for l in range(depth):
        w = w_in[l]
        w_all = jnp.concatenate(
            [w[:, :n_gate0], w[:, n_gate0 + 2 * M_HEADS:],
             jnp.pad(w[:, n_gate0:n_gate0 + 2 * M_HEADS], ((0, 0), (0, GATE_PAD - 2 * M_HEADS)))],
            axis=1).astype(BF16)
        gq = (jnp.tile(q_norm_g[l], A_HEADS) * (A_HEAD_DIM ** -0.5 * LOG2E))[None, :]
        gk = jnp.tile(k_norm_g[l], A_HEADS)[None, :]
        gb = jnp.pad(gate_b[l], (0, GATE_PAD - 2 * M_HEADS))[None, :]
        mq, mv, mo, mz, az, gates, mkt, *qkv_views = _inproj(x2d, norm_g[l][None, :], w_all, bd, gq, gk,
                                                             conv_w[l], conv_b[l][None, :], gb, tril, S)
        hm = _mlstm(mq, mkt, mv, mo, mz, gates, m_norm_g[l][None, :], B, S)
        outs, lses = [], []
        for n, (_, dilation) in enumerate(DILATED_PATTERNS):
            o, lse = _attn_pattern(*qkv_views[3 * n:3 * n + 3], B, S, dilation)
            outs.append(o)
            lses.append(lse)
        x2d = _outproj(x2d, hm, outs, lses, az, w_out[l].astype(BF16), expand)
    return x2d.reshape(B, S, D)
```

```python
import functools

import jax
import jax.numpy as jnp
from jax import lax
from jax.experimental import pallas as pl
from jax.experimental.pallas import tpu as pltpu

F32 = jnp.float32
BF16 = jnp.bfloat16

EPS = 1e-6
D_MODEL = 1024
M_HEADS = 4
M_V_DIM = 256
M_QK_DIM = 128
M_QK_WIDTH = M_HEADS * M_QK_DIM
M_WIDTH = M_HEADS * M_V_DIM
CONV_K = 4
A_HEAD_DIM = 64
A_HEADS = 8
A_WIDTH = A_HEADS * A_HEAD_DIM
A_BLOCK = 128
DILATED_PATTERNS = ((128, 1), (512, 4), (2048, 16))
D_MIX = M_WIDTH + A_WIDTH

LANES = 128
MXU_TILE = 256
GATE_PAD = LANES
N_MAIN = 2 * M_QK_WIDTH + 3 * M_WIDTH + 4 * A_WIDTH
NEG_BIG = -1e30
LOG2E = 1.4426950408889634
LN2 = 0.6931471805599453

V7X_VMEM_BYTES = 64 * 1024 * 1024
VMEM_LIMIT = V7X_VMEM_BYTES * 7 // 8

TM_PROJ = 512
TM_OUT = 1024
A_ROWS_PER_STEP = 2048
M_CHUNK = 256
M_SEQS_PER_STEP = 4


def _sigmoid(x):
    return 1.0 / (1.0 + jnp.exp2(x * (-LOG2E)))


def _split3(x):
    hi = x.astype(BF16)
    r1 = x - hi.astype(F32)
    mid = r1.astype(BF16)
    lo = (r1 - mid.astype(F32)).astype(BF16)
    return hi, mid, lo


def _split2(x):
    hi = x.astype(BF16)
    return hi, (x - hi.astype(F32)).astype(BF16)


def _dot(a, b):
    return jnp.dot(a, b, preferred_element_type=F32)


def _inproj_kernel(tiles_per_seq, x_ref, g_ref, w_ref, bd_ref, gq_ref, gk_ref, cw_ref, cb_ref, gb_ref, tril_ref,
                   q_ref, v_ref, o_ref, z_ref, az_ref, gate_ref, kt_ref,
                   aq_ref, ak_ref, av_ref, aq4_ref, ak4_ref, av4_ref, aq16_ref, ak16_ref, av16_ref, rel, rel4, cbuf, h_ref):
    tm = x_ref.shape[0]
    first = (pl.program_id(0) % tiles_per_seq) == 0

    @pl.when(first)
    def _():
        cbuf[0:8, :] = jnp.zeros((8, cbuf.shape[1]), F32)

    @pl.when(jnp.logical_not(first))
    def _():
        cbuf[0:8, :] = cbuf[tm:tm + 8, :]

    x = x_ref[...]
    h_ref[...] = (x * g_ref[...]).astype(BF16)
    rs = lax.rsqrt(jnp.mean(x * x, axis=-1, keepdims=True) + EPS)

    def proj(c0, c1):
        return _dot(h_ref[...], w_ref[:, c0:c1]) * rs

    def head_norm(a, gain_ref):
        sq = (a * a).astype(BF16)
        bd = bd_ref[...]
        half = bd.shape[0]
        ss = jnp.concatenate([_dot(sq[:, s:s + half], bd) for s in range(0, A_WIDTH, half)], axis=1)
        return a * lax.rsqrt(ss * (1.0 / A_HEAD_DIM) + EPS) * gain_ref[...]

    def emit_views(val, nat_ref, v4_ref, v16_ref, rel, rel4):
        nsl = A_WIDTH // LANES
        nat_ref[...] = val.astype(BF16)
        for sl in range(nsl):
            rel[sl] = val[:, sl * LANES:(sl + 1) * LANES]
        for r4 in range(4):
            for sl in range(nsl):
                blk = rel[sl, pl.ds(r4, tm // 4, stride=4), :]
                rel4[r4 * nsl + sl] = blk
                v4_ref[:, r4 * A_WIDTH + sl * LANES:r4 * A_WIDTH + (sl + 1) * LANES] = blk.astype(BF16)
        for r in range(16):
            for sl in range(nsl):
                c0 = r * A_WIDTH + sl * LANES
                v16_ref[:, c0:c0 + LANES] = rel4[(r % 4) * nsl + sl, pl.ds(r // 4, tm // 16, stride=4), :].astype(BF16)

    c_qk, c_v, c_o, c_z = 0, 2 * M_QK_WIDTH, 2 * M_QK_WIDTH + M_WIDTH, 2 * M_QK_WIDTH + 2 * M_WIDTH
    c_aq = c_z + M_WIDTH
    c_ak, c_av, c_az, c_gate = c_aq + A_WIDTH, c_aq + 2 * A_WIDTH, c_aq + 3 * A_WIDTH, c_aq + 4 * A_WIDTH

    def conv_silu(raw):
        cbuf[8:8 + tm, :] = raw
        conv = cb_ref[...] + cw_ref[0:1, :] * cbuf[5:5 + tm, :]
        for j in range(1, CONV_K):
            conv = conv + cw_ref[j:j + 1, :] * cbuf[5 + j:5 + j + tm, :]
        act = conv * _sigmoid(conv)
        q_ref[...] = (act[:, 0:M_QK_WIDTH] * (M_QK_DIM ** -0.5)).astype(BF16)
        chunk = kt_ref.shape[-1]
        for c in range(tm // chunk):
            for hd in range(M_HEADS):
                c0 = M_QK_WIDTH + hd * M_QK_DIM
                kt_ref[c, hd] = act[c * chunk:(c + 1) * chunk, c0:c0 + M_QK_DIM].T.astype(BF16)

    def gate_logs(raw):
        g = raw + gb_ref[...]
        logf = (jnp.minimum(g, 0.0) - jnp.log1p(jnp.exp(-jnp.abs(g)))) * LOG2E
        hi, mid, lo = _split3(logf)
        tril = tril_ref[...]
        L = tril.shape[0]
        lane = lax.broadcasted_iota(jnp.int32, (L, LANES), 1)
        for c0 in range(0, tm, L):
            cum = _dot(tril, hi[c0:c0 + L]) + _dot(tril, mid[c0:c0 + L]) + _dot(tril, lo[c0:c0 + L])
            gate_ref[c0:c0 + L, :] = jnp.where(lane < M_HEADS, g[c0:c0 + L] * LOG2E, cum)

    def store(ref):
        def epilogue(raw):
            ref[...] = raw.astype(BF16)
        return epilogue

    stages = (
        ((c_v, c_o), store(v_ref)),
        ((c_qk, c_v), conv_silu),
        ((c_gate, c_gate + GATE_PAD), gate_logs),
        ((c_o, c_z), store(o_ref)),
        ((c_aq, c_ak), lambda r: emit_views(head_norm(r, gq_ref), aq_ref, aq4_ref, aq16_ref, rel.at[0], rel4.at[0])),
        ((c_z, c_aq), store(z_ref)),
        ((c_ak, c_av), lambda r: emit_views(head_norm(r, gk_ref), ak_ref, ak4_ref, ak16_ref, rel.at[1], rel4.at[1])),
        ((c_az, c_gate), store(az_ref)),
        ((c_av, c_az), lambda r: emit_views(r, av_ref, av4_ref, av16_ref, rel.at[2], rel4.at[2])),
    )
    for cols, epilogue in stages:
        epilogue(proj(*cols))


def _inproj(x2d, norm_g, w_all, bd, gq, gk, conv_w, conv_b, gate_b, tril, S):
    T = x2d.shape[0]
    tm = TM_PROJ
    assert tm % tril.shape[0] == 0
    const = lambda i: (0, 0)
    row = lambda i: (i, 0)
    chunk = tril.shape[0]
    widths = (M_QK_WIDTH, M_WIDTH, M_WIDTH, M_WIDTH, A_WIDTH)
    out_shape = [jax.ShapeDtypeStruct((T, w), BF16) for w in widths] + [jax.ShapeDtypeStruct((T, GATE_PAD), F32)]
    out_specs = [pl.BlockSpec((tm, w), row) for w in widths] + [pl.BlockSpec((tm, GATE_PAD), row)]
    out_shape += [jax.ShapeDtypeStruct((T // chunk, M_HEADS, M_QK_DIM, chunk), BF16)]
    out_specs += [pl.BlockSpec((tm // chunk, M_HEADS, M_QK_DIM, chunk), lambda i: (i, 0, 0, 0))]
    for _, d in DILATED_PATTERNS:
        out_shape += [jax.ShapeDtypeStruct((T // d, d * A_WIDTH), BF16)] * 3
        out_specs += [pl.BlockSpec((tm // d, d * A_WIDTH), row)] * 3
    return pl.pallas_call(
        functools.partial(_inproj_kernel, S // tm),
        grid=(T // tm,),
        in_specs=[
            pl.BlockSpec((tm, D_MODEL), row),
            pl.BlockSpec((1, D_MODEL), const),
            pl.BlockSpec((D_MODEL, N_MAIN + GATE_PAD), const, pipeline_mode=pl.Buffered(1)),
            pl.BlockSpec(bd.shape, const),
            pl.BlockSpec((1, A_WIDTH), const),
            pl.BlockSpec((1, A_WIDTH), const),
            pl.BlockSpec((CONV_K, 2 * M_QK_WIDTH), const),
            pl.BlockSpec((1, 2 * M_QK_WIDTH), const),
            pl.BlockSpec((1, GATE_PAD), const),
            pl.BlockSpec(tril.shape, const),
        ],
        out_specs=out_specs,
        out_shape=out_shape,
        scratch_shapes=[
            pltpu.VMEM((3, A_WIDTH // LANES, tm, LANES), F32),
            pltpu.VMEM((3, 4 * A_WIDTH // LANES, tm // 4, LANES), F32),
            pltpu.VMEM((tm + 8, 2 * M_QK_WIDTH), F32),
            pltpu.VMEM((tm, D_MODEL), BF16),
        ],
        compiler_params=pltpu.CompilerParams(dimension_semantics=("arbitrary",), vmem_limit_bytes=VMEM_LIMIT),
        name="inproj",
    )(x2d, norm_g, w_all, bd, gq, gk, conv_w, conv_b, gate_b, tril)


def _mlstm_kernel(q_ref, kt_ref, v_ref, o_ref, z_ref, gate_ref, mg_ref, out_ref, c_ref, n_ref, m_ref, cbias_ref):
    nseq, L = q_ref.shape[0], q_ref.shape[1]

    @pl.when(pl.program_id(1) == 0)
    def _():
        c_ref[...] = jnp.zeros(c_ref.shape, F32)
        n_ref[...] = jnp.zeros(n_ref.shape, F32)
        m_ref[...] = jnp.zeros(m_ref.shape, F32)

    t_idx = lax.broadcasted_iota(jnp.int32, (L, L), 0)
    s_idx = lax.broadcasted_iota(jnp.int32, (L, L), 1)
    cbias_ref[...] = jnp.where(s_idx <= t_idx, 0.0, NEG_BIG)
    lane = lax.broadcasted_iota(jnp.int32, (L, LANES), 1)
    ones = jnp.ones((L, LANES), BF16)

    heads = range(M_HEADS)

    def latency_stage(i):
        unit = [i * M_HEADS + h for h in heads]
        q = [q_ref[i, :, h * M_QK_DIM:(h + 1) * M_QK_DIM] for h in heads]
        kt = [kt_ref[i, h] for h in heads]
        v = [v_ref[i, :, h * M_V_DIM:(h + 1) * M_V_DIM] for h in heads]
        m_prev = [m_ref[u] for u in unit]
        c_prev = [c_ref[u] for u in unit]
        n_prev = [n_ref[u] for u in unit]
        qk = [_dot(q[h], kt[h]) for h in heads]
        qc = [_dot(q[h], jnp.concatenate([c_prev[h], n_prev[h]], axis=1).astype(BF16)) for h in heads]

        gb = gate_ref[i]
        gbt = gb.T[0:2 * M_HEADS, :]
        b_row = [gbt[M_HEADS + h:M_HEADS + h + 1, :] for h in heads]
        a_row = [gbt[h:h + 1, :] - b_row[h] for h in heads]

        for h in heads:
            b_last = b_row[h][:, L - 1:L]
            log_state = b_last + a_row[h]
            m_new = jnp.maximum(b_last + m_prev[h], jnp.max(log_state, axis=1, keepdims=True))
            decay = jnp.exp2(b_last + m_prev[h] - m_new)
            kw = kt[h].astype(F32) * jnp.exp2(log_state - m_new)
            c_ref[unit[h]] = decay * c_prev[h] + _dot(kw.astype(BF16), v[h])
            n_ref[unit[h]] = decay * n_prev[h] + jnp.sum(kw, axis=1, keepdims=True)
            m_ref[unit[h]] = m_new

        rep = lambda col: jnp.broadcast_to(col, (L, LANES))
        c_t, floor, w_inter = [], [], []
        for h in heads:
            b_col = rep(jnp.sum(jnp.where(lane == M_HEADS + h, gb, 0.0), axis=1, keepdims=True))
            c_t.append(jnp.maximum(rep(m_prev[h]), rep(jnp.max(a_row[h] + cbias_ref[...], axis=1, keepdims=True))))
            w_inter.append(jnp.exp2(rep(m_prev[h]) - c_t[h]))
            floor.append(jnp.exp2(-(b_col + c_t[h])))
        return v, qk, qc, a_row, c_t, w_inter, floor

    def throughput_stage(i, v, qk, qc, a_row, c_t, w_inter, floor):
        wide = lambda t, width: jnp.concatenate([t] * (width // LANES), axis=1)
        s = [(qk[h] * jnp.exp2((a_row[h] + cbias_ref[...]) - wide(c_t[h], L))).astype(BF16) for h in heads]
        sv = [_dot(s[h], jnp.concatenate([v[h], ones], axis=1)) for h in heads]

        for h in heads:
            num = sv[h][:, :M_V_DIM] + wide(w_inter[h], M_V_DIM) * qc[h][:, :M_V_DIM]
            den = sv[h][:, M_V_DIM:] + w_inter[h] * qc[h][:, M_V_DIM:]
            den = jnp.maximum(jnp.abs(den), floor[h])
            hid = num / jnp.concatenate([den] * (M_V_DIM // LANES), axis=1)
            sl = slice(h * M_V_DIM, (h + 1) * M_V_DIM)
            hg = _sigmoid(o_ref[i, :, sl].astype(F32)) * hid
            hn = hg * lax.rsqrt(jnp.mean(hg * hg, axis=-1, keepdims=True) + EPS) * mg_ref[:, sl]
            zz = z_ref[i, :, sl].astype(F32)
            out_ref[i, :, sl] = (hn * (zz * _sigmoid(zz))).astype(BF16)

    staged = [latency_stage(i) for i in range(nseq)]
    for i in range(nseq):
        throughput_stage(i, *staged[i])


def _mlstm(q, kt, v, o, z, gates, m_norm_g, B, S):
    L = M_CHUNK
    nseq = M_SEQS_PER_STEP
    const = lambda g, s: (0, 0)
    seqs = lambda g, s: (g, 0, s, 0)
    q, v, o, z, gates = [a.reshape(B // nseq, nseq, S, a.shape[-1]) for a in (q, v, o, z, gates)]
    kt = kt.reshape(B // nseq, nseq, S // L, M_HEADS, M_QK_DIM, L)
    operands = [q, kt, v, o, z, gates]
    out = pl.pallas_call(
        _mlstm_kernel,
        grid=(B // nseq, S // L),
        in_specs=[pl.BlockSpec((None, nseq, L, a.shape[-1]), seqs) if a.ndim == 4 else
                  pl.BlockSpec((None, nseq, None, M_HEADS, M_QK_DIM, L), lambda g, s: (g, 0, s, 0, 0, 0))
                  for a in operands]
        + [pl.BlockSpec((1, M_WIDTH), const)],
        out_specs=pl.BlockSpec((None, nseq, L, M_WIDTH), seqs),
        out_shape=jax.ShapeDtypeStruct((B // nseq, nseq, S, M_WIDTH), BF16),
        scratch_shapes=[
            pltpu.VMEM((nseq * M_HEADS, M_QK_DIM, M_V_DIM), F32),
            pltpu.VMEM((nseq * M_HEADS, M_QK_DIM, LANES), F32),
            pltpu.VMEM((nseq * M_HEADS, 1, 1), F32),
            pltpu.VMEM((L, L), F32),
        ],
        compiler_params=pltpu.CompilerParams(dimension_semantics=("arbitrary", "arbitrary"),
                                             vmem_limit_bytes=VMEM_LIMIT),
        name="mlstm",
    )(*operands, m_norm_g)
    return out.reshape(B * S, M_WIDTH)


def _attn_kernel(q_ref, kp_ref, kc_ref, vp_ref, vc_ref, o_ref, lse_ref, bias_ref, bias0_ref):
    TQ = q_ref.shape[0]
    i = pl.program_id(2)
    qi = lax.broadcasted_iota(jnp.int32, (2 * A_BLOCK, 2 * A_BLOCK), 0) % A_BLOCK
    kc = lax.broadcasted_iota(jnp.int32, (2 * A_BLOCK, 2 * A_BLOCK), 1)
    band = (kc >= qi) & (kc <= qi + A_BLOCK)
    bias_ref[...] = jnp.where(band, 0.0, NEG_BIG)
    bias0_ref[...] = jnp.where(band & (kc >= jnp.where(i > 0, 0, A_BLOCK)), 0.0, NEG_BIG)
    lane = lax.broadcasted_iota(jnp.int32, (A_BLOCK, LANES), 1)
    low_half = lane < A_HEAD_DIM
    zero = jnp.zeros((A_BLOCK, LANES), BF16)
    ones = jnp.ones((2 * A_BLOCK, LANES), BF16)

    n_res = q_ref.shape[1] // A_WIDTH
    for res, j in [(res, j) for res in range(n_res) for j in range(TQ // A_BLOCK)]:
        bias = bias0_ref if j == 0 else bias_ref
        rows = slice(j * A_BLOCK, (j + 1) * A_BLOCK)
        mx_tile = jnp.zeros((A_BLOCK, LANES), F32)
        den_tile = jnp.ones((A_BLOCK, LANES), F32)
        for p in range(A_HEADS // 2):
            cols = slice(res * A_WIDTH + p * LANES, res * A_WIDTH + (p + 1) * LANES)
            if j == 0:
                kk = jnp.concatenate([kp_ref[:, cols], kc_ref[0:A_BLOCK, cols]], axis=0)
                vv = jnp.concatenate([vp_ref[:, cols], vc_ref[0:A_BLOCK, cols]], axis=0)
            else:
                kk = kc_ref[(j - 1) * A_BLOCK:(j + 1) * A_BLOCK, cols]
                vv = vc_ref[(j - 1) * A_BLOCK:(j + 1) * A_BLOCK, cols]
            q2 = q_ref[rows, cols]
            qs = jnp.concatenate([jnp.where(low_half, q2, zero), jnp.where(low_half, zero, q2)], axis=0)
            s = lax.dot_general(qs, kk, (((1,), (1,)), ((), ())), preferred_element_type=F32)
            s = s + bias[...]
            mx = jnp.max(s, axis=1, keepdims=True)
            pr = jnp.exp2(s - mx).astype(BF16)
            pv = _dot(pr, jnp.concatenate([vv, ones], axis=1))
            den = pv[:, LANES:]
            out = pv[:, :LANES] / den
            o_ref[rows, cols] = jnp.where(low_half, out[0:A_BLOCK], out[A_BLOCK:]).astype(BF16)
            mx_tile = jnp.where(lane == 2 * p, mx[0:A_BLOCK], jnp.where(lane == 2 * p + 1, mx[A_BLOCK:], mx_tile))
            den_tile = jnp.where(lane == 2 * p, den[0:A_BLOCK], jnp.where(lane == 2 * p + 1, den[A_BLOCK:], den_tile))
        lse_ref[rows, res * LANES:(res + 1) * LANES] = mx_tile * LN2 + jnp.log(den_tile)


def _attn_pattern(qv, kv, vv, B, S, dilation):
    d = dilation
    L = S // d
    TQ = min(A_ROWS_PER_STEP, L)
    nq = L // TQ
    r = TQ // A_BLOCK
    n_res = min(d, A_ROWS_PER_STEP // TQ)
    cur = lambda b, rr, i: (b * nq + i, rr)
    prev = lambda b, rr, i: (b * (L // A_BLOCK) + jnp.maximum(i * r - 1, 0), rr)
    return pl.pallas_call(
        _attn_kernel,
        grid=(B, d // n_res, nq),
        in_specs=[
            pl.BlockSpec((TQ, n_res * A_WIDTH), cur),
            pl.BlockSpec((A_BLOCK, n_res * A_WIDTH), prev),
            pl.BlockSpec((TQ, n_res * A_WIDTH), cur),
            pl.BlockSpec((A_BLOCK, n_res * A_WIDTH), prev),
            pl.BlockSpec((TQ, n_res * A_WIDTH), cur),
        ],
        out_specs=[
            pl.BlockSpec((TQ, n_res * A_WIDTH), cur),
            pl.BlockSpec((TQ, n_res * LANES), cur),
        ],
        out_shape=[
            jax.ShapeDtypeStruct((B * L, d * A_WIDTH), BF16),
            jax.ShapeDtypeStruct((B * L, d * LANES), F32),
        ],
        scratch_shapes=[
            pltpu.VMEM((2 * A_BLOCK, 2 * A_BLOCK), F32),
            pltpu.VMEM((2 * A_BLOCK, 2 * A_BLOCK), F32),
        ],
        compiler_params=pltpu.CompilerParams(dimension_semantics=("arbitrary", "arbitrary", "arbitrary"),
                                             vmem_limit_bytes=VMEM_LIMIT),
        name=f"attn_d{d}",
    )(qv, kv, kv, vv, vv)


def _outproj_kernel(x_ref, hm_ref, o1_ref, o4_ref, o16_ref, l1_ref, l4_ref, l16_ref, az_ref, w_ref, ex_ref,
                    out_ref, o4n, o16n, l4n, l16n, ha_scr):
    tm = x_ref.shape[0]
    out_ref[...] = x_ref[...] + _dot(hm_ref[...], w_ref[0:M_WIDTH, :])
    for d, o_ref, l_ref, o_nat, l_nat in ((4, o4_ref, l4_ref, o4n, l4n), (16, o16_ref, l16_ref, o16n, l16n)):
        for r in range(d):
            l_nat[pl.ds(r, tm // d, stride=d), :] = l_ref[:, r * LANES:(r + 1) * LANES]
            for sl in range(A_WIDTH // LANES):
                c0 = r * A_WIDTH + sl * LANES
                o_nat[sl, pl.ds(r, tm // d, stride=d), :] = o_ref[:, c0:c0 + LANES].astype(F32)

    l1, l2, l3 = l1_ref[...], l4n[...], l16n[...]
    mx = jnp.maximum(jnp.maximum(l1, l2), l3)
    e1, e2, e3 = jnp.exp(l1 - mx), jnp.exp(l2 - mx), jnp.exp(l3 - mx)
    tot = e1 + e2 + e3
    lane = lax.broadcasted_iota(jnp.int32, (tm, LANES), 1)
    packed = None
    for n, w in enumerate((e1 / tot, e2 / tot, e3 / tot)):
        hi = w.astype(BF16).astype(F32)
        for part in (hi, w - hi):
            slot = 2 * n + (0 if part is hi else 1)
            packed = part if packed is None else jnp.where(lane < A_HEADS * slot, packed,
                                                             pltpu.roll(part, A_HEADS * slot, axis=1))
    wexp = _dot(packed.astype(BF16), ex_ref[...])
    for sl in range(A_WIDTH // LANES):
        cols = slice(sl * LANES, (sl + 1) * LANES)
        pats = (o1_ref[:, cols].astype(F32), o4n[sl], o16n[sl])
        wcol = lambda n: wexp[:, n * A_WIDTH + sl * LANES:n * A_WIDTH + (sl + 1) * LANES]
        acc = wcol(0) * pats[0] + wcol(1) * pats[1] + wcol(2) * pats[2]
        zz = az_ref[:, cols].astype(F32)
        ha_scr[:, cols] = (acc * (zz * _sigmoid(zz))).astype(BF16)

    out_ref[...] += _dot(ha_scr[...], w_ref[M_WIDTH:D_MIX, :])


def _outproj(x2d, hm, outs, lses, az, w_out, expand):
    T = x2d.shape[0]
    tm = TM_OUT
    row = lambda i: (i, 0)
    const = lambda i: (0, 0)
    dils = [d for _, d in DILATED_PATTERNS]
    return pl.pallas_call(
        _outproj_kernel,
        grid=(T // tm,),
        in_specs=[pl.BlockSpec((tm, D_MODEL), row), pl.BlockSpec((tm, M_WIDTH), row)]
        + [pl.BlockSpec((tm // d, d * A_WIDTH), row) for d in dils]
        + [pl.BlockSpec((tm // d, d * LANES), row) for d in dils]
        + [pl.BlockSpec((tm, A_WIDTH), row), pl.BlockSpec((D_MIX, D_MODEL), const),
           pl.BlockSpec(expand.shape, const)],
        out_specs=pl.BlockSpec((tm, D_MODEL), row),
        out_shape=jax.ShapeDtypeStruct((T, D_MODEL), F32),
        scratch_shapes=[
            pltpu.VMEM((A_WIDTH // LANES, tm, LANES), F32),
            pltpu.VMEM((A_WIDTH // LANES, tm, LANES), F32),
            pltpu.VMEM((tm, LANES), F32),
            pltpu.VMEM((tm, LANES), F32),
            pltpu.VMEM((tm, A_WIDTH), BF16),
        ],
        compiler_params=pltpu.CompilerParams(dimension_semantics=("arbitrary",), vmem_limit_bytes=VMEM_LIMIT),
        name="outproj",
    )(x2d, hm, *outs, *lses, az, w_out, expand)


def _constants():
    idx = jnp.arange(MXU_TILE)
    bd = (idx[:, None] // A_HEAD_DIM == idx[None, :] // A_HEAD_DIM).astype(BF16)
    t = jnp.arange(M_CHUNK)
    tril = (t[None, :] <= t[:, None]).astype(BF16)
    row = jnp.arange(LANES)[:, None]
    col = jnp.arange(len(DILATED_PATTERNS) * A_WIDTH)[None, :]
    expand = ((row // (2 * A_HEADS) == col // A_WIDTH) & (row % A_HEADS == (col % A_WIDTH) // A_HEAD_DIM)
              & (row < 2 * A_HEADS * len(DILATED_PATTERNS))).astype(BF16)
    return bd, tril, expand


def kernel(x, norm_g, w_in, gate_b, conv_w, conv_b, m_norm_g, q_norm_g, k_norm_g, w_out):
    B, S, D = x.shape
    depth = norm_g.shape[0]
    assert D == D_MODEL and S % M_CHUNK == 0 and S % TM_PROJ == 0 and B % M_SEQS_PER_STEP == 0
    assert (B * S) % TM_OUT == 0 and TM_OUT % (16 * 16) == 0
    assert tuple(d for _, d in DILATED_PATTERNS) == (1, 4, 16)
    for window, dilation in DILATED_PATTERNS:
        assert window == dilation * A_BLOCK and S % (dilation * A_BLOCK) == 0 and TM_PROJ % (16 * dilation) == 0
    bd, tril, expand = _constants()
    n_gate0 = 2 * M_QK_WIDTH + 3 * M_WIDTH
    x2d = x.reshape(B * S, D)
    for l in range(depth):
        w = w_in[l]
        w_all = jnp.concatenate(
            [w[:, :n_gate0], w[:, n_gate0 + 2 * M_HEADS:],
             jnp.pad(w[:, n_gate0:n_gate0 + 2 * M_HEADS], ((0, 0), (0, GATE_PAD - 2 * M_HEADS)))],
            axis=1).astype(BF16)
        gq = (jnp.tile(q_norm_g[l], A_HEADS) * (A_HEAD_DIM ** -0.5 * LOG2E))[None, :]
        gk = jnp.tile(k_norm_g[l], A_HEADS)[None, :]
        gb = jnp.pad(gate_b[l], (0, GATE_PAD - 2 * M_HEADS))[None, :]
        mq, mv, mo, mz, az, gates, mkt, *qkv_views = _inproj(x2d, norm_g[l][None, :], w_all, bd, gq, gk,
                                                             conv_w[l], conv_b[l][None, :], gb, tril, S)
        hm = _mlstm(mq, mkt, mv, mo, mz, gates, m_norm_g[l][None, :], B, S)
        outs, lses = [], []
        for n, (_, dilation) in enumerate(DILATED_PATTERNS):
            o, lse = _attn_pattern(*qkv_views[3 * n:3 * n + 3], B, S, dilation)
            outs.append(o)
            lses.append(lse)
        x2d = _outproj(x2d, hm, outs, lses, az, w_out[l].astype(BF16), expand)
    return x2d.reshape(B, S, D)
```

```python
import functools

import jax
import jax.numpy as jnp
from jax import lax
from jax.experimental import pallas as pl
from jax.experimental.pallas import tpu as pltpu

F32 = jnp.float32
BF16 = jnp.bfloat16

EPS = 1e-6
D_MODEL = 1024
M_HEADS = 4
M_V_DIM = 256
M_QK_DIM = 128
M_QK_WIDTH = M_HEADS * M_QK_DIM
M_WIDTH = M_HEADS * M_V_DIM
CONV_K = 4
A_HEAD_DIM = 64
A_HEADS = 8
A_WIDTH = A_HEADS * A_HEAD_DIM
A_BLOCK = 128
DILATED_PATTERNS = ((128, 1), (512, 4), (2048, 16))
D_MIX = M_WIDTH + A_WIDTH

LANES = 128
MXU_TILE = 256
GATE_PAD = LANES
N_MAIN = 2 * M_QK_WIDTH + 3 * M_WIDTH + 4 * A_WIDTH
NEG_BIG = -1e30
LOG2E = 1.4426950408889634
LN2 = 0.6931471805599453

V7X_VMEM_BYTES = 64 * 1024 * 1024
VMEM_LIMIT = V7X_VMEM_BYTES * 7 // 8

TM_PROJ = 512
TM_OUT = 1024
A_ROWS_PER_STEP = 2048
M_CHUNK = 256
M_SEQS_PER_STEP = 4


def _sigmoid(x):
    return 1.0 / (1.0 + jnp.exp2(x * (-LOG2E)))


def _split3(x):
    hi = x.astype(BF16)
    r1 = x - hi.astype(F32)
    mid = r1.astype(BF16)
    lo = (r1 - mid.astype(F32)).astype(BF16)
    return hi, mid, lo


def _dot(a, b):
    return jnp.dot(a, b, preferred_element_type=F32)


def _inproj_kernel(tiles_per_seq, x_ref, g_ref, w_ref, bd_ref, gq_ref, gk_ref, cw_ref, cb_ref, gb_ref, tril_ref,
                   q_ref, v_ref, o_ref, z_ref, az_ref, gate_ref, kt_ref,
                   aq_ref, ak_ref, av_ref, aq4_ref, ak4_ref, av4_ref, aq16_ref, ak16_ref, av16_ref, rel, rel4, cbuf):
    tm = x_ref.shape[0]
    first = (pl.program_id(0) % tiles_per_seq) == 0

    @pl.when(first)
    def _():
        cbuf[0:8, :] = jnp.zeros((8, cbuf.shape[1]), F32)

    @pl.when(jnp.logical_not(first))
    def _():
        cbuf[0:8, :] = cbuf[tm:tm + 8, :]

    x = x_ref[...]
    h = (x * g_ref[...]).astype(BF16)
    rs = lax.rsqrt(jnp.mean(x * x, axis=-1, keepdims=True) + EPS)

    def proj(c0, c1):
        return _dot(h, w_ref[:, c0:c1]) * rs

    def head_norm(a, gain_ref):
        sq = (a * a).astype(BF16)
        bd = bd_ref[...]
        half = bd.shape[0]
        ss = jnp.concatenate([_dot(sq[:, s:s + half], bd) for s in range(0, A_WIDTH, half)], axis=1)
        return a * lax.rsqrt(ss * (1.0 / A_HEAD_DIM) + EPS) * gain_ref[...]

    def emit_views(val, nat_ref, v4_ref, v16_ref, rel, rel4):
        nsl = A_WIDTH // LANES
        nat_ref[...] = val.astype(BF16)
        for sl in range(nsl):
            rel[sl] = val[:, sl * LANES:(sl + 1) * LANES]
        for r4 in range(4):
            for sl in range(nsl):
                blk = rel[sl, pl.ds(r4, tm // 4, stride=4), :]
                rel4[r4 * nsl + sl] = blk
                v4_ref[:, r4 * A_WIDTH + sl * LANES:r4 * A_WIDTH + (sl + 1) * LANES] = blk.astype(BF16)
        for r in range(16):
            for sl in range(nsl):
                c0 = r * A_WIDTH + sl * LANES
                v16_ref[:, c0:c0 + LANES] = rel4[(r % 4) * nsl + sl, pl.ds(r // 4, tm // 16, stride=4), :].astype(BF16)

    c_qk, c_v, c_o, c_z = 0, 2 * M_QK_WIDTH, 2 * M_QK_WIDTH + M_WIDTH, 2 * M_QK_WIDTH + 2 * M_WIDTH
    c_aq = c_z + M_WIDTH
    c_ak, c_av, c_az, c_gate = c_aq + A_WIDTH, c_aq + 2 * A_WIDTH, c_aq + 3 * A_WIDTH, c_aq + 4 * A_WIDTH

    def conv_silu(raw):
        cbuf[8:8 + tm, :] = raw
        conv = cb_ref[...] + cw_ref[0:1, :] * cbuf[5:5 + tm, :]
        for j in range(1, CONV_K):
            conv = conv + cw_ref[j:j + 1, :] * cbuf[5 + j:5 + j + tm, :]
        act = conv * _sigmoid(conv)
        q_ref[...] = (act[:, 0:M_QK_WIDTH] * (M_QK_DIM ** -0.5)).astype(BF16)
        chunk = kt_ref.shape[-1]
        for c in range(tm // chunk):
            for hd in range(M_HEADS):
                c0 = M_QK_WIDTH + hd * M_QK_DIM
                kt_ref[c, hd] = act[c * chunk:(c + 1) * chunk, c0:c0 + M_QK_DIM].T.astype(BF16)

    def gate_logs(raw):
        g = raw + gb_ref[...]
        logf = (jnp.minimum(g, 0.0) - jnp.log1p(jnp.exp(-jnp.abs(g)))) * LOG2E
        hi, mid, lo = _split3(logf)
        tril = tril_ref[...]
        L = tril.shape[0]
        lane = lax.broadcasted_iota(jnp.int32, (L, LANES), 1)
        for c0 in range(0, tm, L):
            cum = _dot(tril, hi[c0:c0 + L]) + _dot(tril, mid[c0:c0 + L]) + _dot(tril, lo[c0:c0 + L])
            gate_ref[c0:c0 + L, :] = jnp.where(lane < M_HEADS, g[c0:c0 + L] * LOG2E, cum)

    def store(ref):
        def epilogue(raw):
            ref[...] = raw.astype(BF16)
        return epilogue

    stages = (
        ((c_v, c_o), store(v_ref)),
        ((c_qk, c_v), conv_silu),
        ((c_gate, c_gate + GATE_PAD), gate_logs),
        ((c_o, c_z), store(o_ref)),
        ((c_aq, c_ak), lambda r: emit_views(head_norm(r, gq_ref), aq_ref, aq4_ref, aq16_ref, rel.at[0], rel4.at[0])),
        ((c_z, c_aq), store(z_ref)),
        ((c_ak, c_av), lambda r: emit_views(head_norm(r, gk_ref), ak_ref, ak4_ref, ak16_ref, rel.at[1], rel4.at[1])),
        ((c_az, c_gate), store(az_ref)),
        ((c_av, c_az), lambda r: emit_views(r, av_ref, av4_ref, av16_ref, rel.at[2], rel4.at[2])),
    )
    for cols, epilogue in stages:
        epilogue(proj(*cols))


def _inproj(x2d, norm_g, w_all, bd, gq, gk, conv_w, conv_b, gate_b, tril, S):
    T = x2d.shape[0]
    tm = TM_PROJ
    assert tm % tril.shape[0] == 0
    const = lambda i: (0, 0)
    row = lambda i: (i, 0)
    chunk = tril.shape[0]
    widths = (M_QK_WIDTH, M_WIDTH, M_WIDTH, M_WIDTH, A_WIDTH)
    out_shape = [jax.ShapeDtypeStruct((T, w), BF16) for w in widths] + [jax.ShapeDtypeStruct((T, GATE_PAD), F32)]
    out_specs = [pl.BlockSpec((tm, w), row) for w in widths] + [pl.BlockSpec((tm, GATE_PAD), row)]
    out_shape += [jax.ShapeDtypeStruct((T // chunk, M_HEADS, M_QK_DIM, chunk), BF16)]
    out_specs += [pl.BlockSpec((tm // chunk, M_HEADS, M_QK_DIM, chunk), lambda i: (i, 0, 0, 0))]
    for _, d in DILATED_PATTERNS:
        out_shape += [jax.ShapeDtypeStruct((T // d, d * A_WIDTH), BF16)] * 3
        out_specs += [pl.BlockSpec((tm // d, d * A_WIDTH), row)] * 3
    return pl.pallas_call(
        functools.partial(_inproj_kernel, S // tm),
        grid=(T // tm,),
        in_specs=[
            pl.BlockSpec((tm, D_MODEL), row),
            pl.BlockSpec((1, D_MODEL), const),
            pl.BlockSpec((D_MODEL, N_MAIN + GATE_PAD), const, pipeline_mode=pl.Buffered(1)),
            pl.BlockSpec(bd.shape, const),
            pl.BlockSpec((1, A_WIDTH), const),
            pl.BlockSpec((1, A_WIDTH), const),
            pl.BlockSpec((CONV_K, 2 * M_QK_WIDTH), const),
            pl.BlockSpec((1, 2 * M_QK_WIDTH), const),
            pl.BlockSpec((1, GATE_PAD), const),
            pl.BlockSpec(tril.shape, const),
        ],
        out_specs=out_specs,
        out_shape=out_shape,
        scratch_shapes=[
            pltpu.VMEM((3, A_WIDTH // LANES, tm, LANES), F32),
            pltpu.VMEM((3, 4 * A_WIDTH // LANES, tm // 4, LANES), F32),
            pltpu.VMEM((tm + 8, 2 * M_QK_WIDTH), F32),
        ],
        compiler_params=pltpu.CompilerParams(dimension_semantics=("arbitrary",), vmem_limit_bytes=VMEM_LIMIT),
        name="inproj",
    )(x2d, norm_g, w_all, bd, gq, gk, conv_w, conv_b, gate_b, tril)


def _mlstm_kernel(q_ref, kt_ref, v_ref, o_ref, z_ref, gate_ref, mg_ref, out_ref, c_ref, n_ref, m_ref, cbias_ref):
    nseq, L = q_ref.shape[0], q_ref.shape[1]

    @pl.when(pl.program_id(1) == 0)
    def _():
        c_ref[...] = jnp.zeros(c_ref.shape, F32)
        n_ref[...] = jnp.zeros(n_ref.shape, F32)
        m_ref[...] = jnp.zeros(m_ref.shape, F32)

    t_idx = lax.broadcasted_iota(jnp.int32, (L, L), 0)
    s_idx = lax.broadcasted_iota(jnp.int32, (L, L), 1)
    cbias_ref[...] = jnp.where(s_idx <= t_idx, 0.0, NEG_BIG)
    lane = lax.broadcasted_iota(jnp.int32, (L, LANES), 1)
    ones = jnp.ones((L, LANES), BF16)

    heads = range(M_HEADS)

    def latency_stage(i):
        unit = [i * M_HEADS + h for h in heads]
        q = [q_ref[i, :, h * M_QK_DIM:(h + 1) * M_QK_DIM] for h in heads]
        kt = [kt_ref[i, h] for h in heads]
        v = [v_ref[i, :, h * M_V_DIM:(h + 1) * M_V_DIM] for h in heads]
        m_prev = [m_ref[u] for u in unit]
        c_prev = [c_ref[u] for u in unit]
        n_prev = [n_ref[u] for u in unit]
        qk = [_dot(q[h], kt[h]) for h in heads]
        qc = [_dot(q[h], jnp.concatenate([c_prev[h], n_prev[h]], axis=1).astype(BF16)) for h in heads]

        gb = gate_ref[i]
        gbt = gb.T[0:2 * M_HEADS, :]
        b_row = [gbt[M_HEADS + h:M_HEADS + h + 1, :] for h in heads]
        a_row = [gbt[h:h + 1, :] - b_row[h] for h in heads]

        for h in heads:
            b_last = b_row[h][:, L - 1:L]
            log_state = b_last + a_row[h]
            m_new = jnp.maximum(b_last + m_prev[h], jnp.max(log_state, axis=1, keepdims=True))
            decay = jnp.exp2(b_last + m_prev[h] - m_new)
            kw = kt[h].astype(F32) * jnp.exp2(log_state - m_new)
            c_ref[unit[h]] = decay * c_prev[h] + _dot(kw.astype(BF16), v[h])
            n_ref[unit[h]] = decay * n_prev[h] + jnp.sum(kw, axis=1, keepdims=True)
            m_ref[unit[h]] = m_new

        rep = lambda col: jnp.broadcast_to(col, (L, LANES))
        c_t, floor, w_inter = [], [], []
        for h in heads:
            b_col = rep(jnp.sum(jnp.where(lane == M_HEADS + h, gb, 0.0), axis=1, keepdims=True))
            c_t.append(jnp.maximum(rep(m_prev[h]), rep(jnp.max(a_row[h] + cbias_ref[...], axis=1, keepdims=True))))
            w_inter.append(jnp.exp2(rep(m_prev[h]) - c_t[h]))
            floor.append(jnp.exp2(-(b_col + c_t[h])))
        return v, qk, qc, a_row, c_t, w_inter, floor

    def throughput_stage(i, v, qk, qc, a_row, c_t, w_inter, floor):
        wide = lambda t, width: jnp.concatenate([t] * (width // LANES), axis=1)
        s = [(qk[h] * jnp.exp2((a_row[h] + cbias_ref[...]) - wide(c_t[h], L))).astype(BF16) for h in heads]
        sv = [_dot(s[h], jnp.concatenate([v[h], ones], axis=1)) for h in heads]

        for h in heads:
            num = sv[h][:, :M_V_DIM] + wide(w_inter[h], M_V_DIM) * qc[h][:, :M_V_DIM]
            den = sv[h][:, M_V_DIM:] + w_inter[h] * qc[h][:, M_V_DIM:]
            den = jnp.maximum(jnp.abs(den), floor[h])
            hid = num / jnp.concatenate([den] * (M_V_DIM // LANES), axis=1)
            sl = slice(h * M_V_DIM, (h + 1) * M_V_DIM)
            hg = _sigmoid(o_ref[i, :, sl].astype(F32)) * hid
            hn = hg * lax.rsqrt(jnp.mean(hg * hg, axis=-1, keepdims=True) + EPS) * mg_ref[:, sl]
            zz = z_ref[i, :, sl].astype(F32)
            out_ref[i, :, sl] = (hn * (zz * _sigmoid(zz))).astype(BF16)

    staged = [latency_stage(i) for i in range(nseq)]
    for i in range(nseq):
        throughput_stage(i, *staged[i])


def _mlstm(q, kt, v, o, z, gates, m_norm_g, B, S):
    L = M_CHUNK
    nseq = M_SEQS_PER_STEP
    const = lambda g, s: (0, 0)
    seqs = lambda g, s: (g, 0, s, 0)
    q, v, o, z, gates = [a.reshape(B // nseq, nseq, S, a.shape[-1]) for a in (q, v, o, z, gates)]
    kt = kt.reshape(B // nseq, nseq, S // L, M_HEADS, M_QK_DIM, L)
    operands = [q, kt, v, o, z, gates]
    out = pl.pallas_call(
        _mlstm_kernel,
        grid=(B // nseq, S // L),
        in_specs=[pl.BlockSpec((None, nseq, L, a.shape[-1]), seqs) if a.ndim == 4 else
                  pl.BlockSpec((None, nseq, None, M_HEADS, M_QK_DIM, L), lambda g, s: (g, 0, s, 0, 0, 0))
                  for a in operands]
        + [pl.BlockSpec((1, M_WIDTH), const)],
        out_specs=pl.BlockSpec((None, nseq, L, M_WIDTH), seqs),
        out_shape=jax.ShapeDtypeStruct((B // nseq, nseq, S, M_WIDTH), BF16),
        scratch_shapes=[
            pltpu.VMEM((nseq * M_HEADS, M_QK_DIM, M_V_DIM), F32),
            pltpu.VMEM((nseq * M_HEADS, M_QK_DIM, LANES), F32),
            pltpu.VMEM((nseq * M_HEADS, 1, 1), F32),
            pltpu.VMEM((L, L), F32),
        ],
        compiler_params=pltpu.CompilerParams(dimension_semantics=("arbitrary", "arbitrary"),
                                             vmem_limit_bytes=VMEM_LIMIT),
        name="mlstm",
    )(*operands, m_norm_g)
    return out.reshape(B * S, M_WIDTH)


def _attn_kernel(q_ref, kp_ref, kc_ref, vp_ref, vc_ref, o_ref, lse_ref, bias_ref, bias0_ref):
    TQ = q_ref.shape[0]
    i = pl.program_id(2)
    qi = lax.broadcasted_iota(jnp.int32, (2 * A_BLOCK, 2 * A_BLOCK), 0) % A_BLOCK
    kc = lax.broadcasted_iota(jnp.int32, (2 * A_BLOCK, 2 * A_BLOCK), 1)
    band = (kc >= qi) & (kc <= qi + A_BLOCK)
    bias_ref[...] = jnp.where(band, 0.0, NEG_BIG)
    bias0_ref[...] = jnp.where(band & (kc >= jnp.where(i > 0, 0, A_BLOCK)), 0.0, NEG_BIG)
    lane = lax.broadcasted_iota(jnp.int32, (A_BLOCK, LANES), 1)
    low_half = lane < A_HEAD_DIM
    zero = jnp.zeros((A_BLOCK, LANES), BF16)
    ones = jnp.ones((2 * A_BLOCK, LANES), BF16)

    n_res = q_ref.shape[1] // A_WIDTH
    for res, j in [(res, j) for res in range(n_res) for j in range(TQ // A_BLOCK)]:
        bias = bias0_ref if j == 0 else bias_ref
        rows = slice(j * A_BLOCK, (j + 1) * A_BLOCK)
        mx_tile = jnp.zeros((A_BLOCK, LANES), F32)
        den_tile = jnp.ones((A_BLOCK, LANES), F32)
        for p in range(A_HEADS // 2):
            cols = slice(res * A_WIDTH + p * LANES, res * A_WIDTH + (p + 1) * LANES)
            if j == 0:
                kk = jnp.concatenate([kp_ref[:, cols], kc_ref[0:A_BLOCK, cols]], axis=0)
                vv = jnp.concatenate([vp_ref[:, cols], vc_ref[0:A_BLOCK, cols]], axis=0)
            else:
                kk = kc_ref[(j - 1) * A_BLOCK:(j + 1) * A_BLOCK, cols]
                vv = vc_ref[(j - 1) * A_BLOCK:(j + 1) * A_BLOCK, cols]
            q2 = q_ref[rows, cols]
            qs = jnp.concatenate([jnp.where(low_half, q2, zero), jnp.where(low_half, zero, q2)], axis=0)
            s = lax.dot_general(qs, kk, (((1,), (1,)), ((), ())), preferred_element_type=F32)
            s = s + bias[...]
            mx = jnp.max(s, axis=1, keepdims=True)
            pr = jnp.exp2(s - mx).astype(BF16)
            pv = _dot(pr, jnp.concatenate([vv, ones], axis=1))
            den = pv[:, LANES:]
            out = pv[:, :LANES] / den
            o_ref[rows, cols] = jnp.where(low_half, out[0:A_BLOCK], out[A_BLOCK:]).astype(BF16)
            mx_tile = jnp.where(lane == 2 * p, mx[0:A_BLOCK], jnp.where(lane == 2 * p + 1, mx[A_BLOCK:], mx_tile))
            den_tile = jnp.where(lane == 2 * p, den[0:A_BLOCK], jnp.where(lane == 2 * p + 1, den[A_BLOCK:], den_tile))
        lse_ref[rows, res * LANES:(res + 1) * LANES] = mx_tile * LN2 + jnp.log(den_tile)


def _attn_pattern(qv, kv, vv, B, S, dilation):
    d = dilation
    L = S // d
    TQ = min(A_ROWS_PER_STEP, L)
    nq = L // TQ
    r = TQ // A_BLOCK
    n_res = min(d, A_ROWS_PER_STEP // TQ)
    cur = lambda b, rr, i: (b * nq + i, rr)
    prev = lambda b, rr, i: (b * (L // A_BLOCK) + jnp.maximum(i * r - 1, 0), rr)
    return pl.pallas_call(
        _attn_kernel,
        grid=(B, d // n_res, nq),
        in_specs=[
            pl.BlockSpec((TQ, n_res * A_WIDTH), cur),
            pl.BlockSpec((A_BLOCK, n_res * A_WIDTH), prev),
            pl.BlockSpec((TQ, n_res * A_WIDTH), cur),
            pl.BlockSpec((A_BLOCK, n_res * A_WIDTH), prev),
            pl.BlockSpec((TQ, n_res * A_WIDTH), cur),
        ],
        out_specs=[
            pl.BlockSpec((TQ, n_res * A_WIDTH), cur),
            pl.BlockSpec((TQ, n_res * LANES), cur),
        ],
        out_shape=[
            jax.ShapeDtypeStruct((B * L, d * A_WIDTH), BF16),
            jax.ShapeDtypeStruct((B * L, d * LANES), F32),
        ],
        scratch_shapes=[
            pltpu.VMEM((2 * A_BLOCK, 2 * A_BLOCK), F32),
            pltpu.VMEM((2 * A_BLOCK, 2 * A_BLOCK), F32),
        ],
        compiler_params=pltpu.CompilerParams(dimension_semantics=("arbitrary", "arbitrary", "arbitrary"),
                                             vmem_limit_bytes=VMEM_LIMIT),
        name=f"attn_d{d}",
    )(qv, kv, kv, vv, vv)


def _outproj_kernel(x_ref, hm_ref, o1_ref, o4_ref, o16_ref, l1_ref, l4_ref, l16_ref, az_ref, w_ref, ex_ref,
                    out_ref, o4n, o16n, l4n, l16n, ha_scr):
    tm = x_ref.shape[0]
    out_ref[...] = x_ref[...] + _dot(hm_ref[...], w_ref[0:M_WIDTH, :])
    for d, o_ref, l_ref, o_nat, l_nat in ((4, o4_ref, l4_ref, o4n, l4n), (16, o16_ref, l16_ref, o16n, l16n)):
        for r in range(d):
            l_nat[pl.ds(r, tm // d, stride=d), :] = l_ref[:, r * LANES:(r + 1) * LANES]
            for sl in range(A_WIDTH // LANES):
                c0 = r * A_WIDTH + sl * LANES
                o_nat[sl, pl.ds(r, tm // d, stride=d), :] = o_ref[:, c0:c0 + LANES].astype(F32)

    l1, l2, l3 = l1_ref[...], l4n[...], l16n[...]
    mx = jnp.maximum(jnp.maximum(l1, l2), l3)
    e1, e2, e3 = jnp.exp(l1 - mx), jnp.exp(l2 - mx), jnp.exp(l3 - mx)
    tot = e1 + e2 + e3
    lane = lax.broadcasted_iota(jnp.int32, (tm, LANES), 1)
    packed = None
    for n, w in enumerate((e1 / tot, e2 / tot, e3 / tot)):
        hi = w.astype(BF16).astype(F32)
        for part in (hi, w - hi):
            slot = 2 * n + (0 if part is hi else 1)
            packed = part if packed is None else jnp.where(lane < A_HEADS * slot, packed,
                                                             pltpu.roll(part, A_HEADS * slot, axis=1))
    wexp = _dot(packed.astype(BF16), ex_ref[...])
    for sl in range(A_WIDTH // LANES):
        cols = slice(sl * LANES, (sl + 1) * LANES)
        pats = (o1_ref[:, cols].astype(F32), o4n[sl], o16n[sl])
        wcol = lambda n: wexp[:, n * A_WIDTH + sl * LANES:n * A_WIDTH + (sl + 1) * LANES]
        acc = wcol(0) * pats[0] + wcol(1) * pats[1] + wcol(2) * pats[2]
        zz = az_ref[:, cols].astype(F32)
        ha_scr[:, cols] = (acc * (zz * _sigmoid(zz))).astype(BF16)

    out_ref[...] += _dot(ha_scr[...], w_ref[M_WIDTH:D_MIX, :])


def _outproj(x2d, hm, outs, lses, az, w_out, expand):
    T = x2d.shape[0]
    tm = TM_OUT
    row = lambda i: (i, 0)
    const = lambda i: (0, 0)
    dils = [d for _, d in DILATED_PATTERNS]
    return pl.pallas_call(
        _outproj_kernel,
        grid=(T // tm,),
        in_specs=[pl.BlockSpec((tm, D_MODEL), row), pl.BlockSpec((tm, M_WIDTH), row)]
        + [pl.BlockSpec((tm // d, d * A_WIDTH), row) for d in dils]
        + [pl.BlockSpec((tm // d, d * LANES), row) for d in dils]
        + [pl.BlockSpec((tm, A_WIDTH), row), pl.BlockSpec((D_MIX, D_MODEL), const),
           pl.BlockSpec(expand.shape, const)],
        out_specs=pl.BlockSpec((tm, D_MODEL), row),
        out_shape=jax.ShapeDtypeStruct((T, D_MODEL), F32),
        scratch_shapes=[
            pltpu.VMEM((A_WIDTH // LANES, tm, LANES), F32),
            pltpu.VMEM((A_WIDTH // LANES, tm, LANES), F32),
            pltpu.VMEM((tm, LANES), F32),
            pltpu.VMEM((tm, LANES), F32),
            pltpu.VMEM((tm, A_WIDTH), BF16),
        ],
        compiler_params=pltpu.CompilerParams(dimension_semantics=("arbitrary",), vmem_limit_bytes=VMEM_LIMIT),
        name="outproj",
    )(x2d, hm, *outs, *lses, az, w_out, expand)


def _constants():
    idx = jnp.arange(MXU_TILE)
    bd = (idx[:, None] // A_HEAD_DIM == idx[None, :] // A_HEAD_DIM).astype(BF16)
    t = jnp.arange(M_CHUNK)
    tril = (t[None, :] <= t[:, None]).astype(BF16)
    row = jnp.arange(LANES)[:, None]
    col = jnp.arange(len(DILATED_PATTERNS) * A_WIDTH)[None, :]
    expand = ((row // (2 * A_HEADS) == col // A_WIDTH) & (row % A_HEADS == (col % A_WIDTH) // A_HEAD_DIM)
              & (row < 2 * A_HEADS * len(DILATED_PATTERNS))).astype(BF16)
    return bd, tril, expand


def kernel(x, norm_g, w_in, gate_b, conv_w, conv_b, m_norm_g, q_norm_g, k_norm_g, w_out):
    B, S, D = x.shape
    depth = norm_g.shape[0]
    assert D == D_MODEL and S % M_CHUNK == 0 and S % TM_PROJ == 0 and B % M_SEQS_PER_STEP == 0
    assert (B * S) % TM_OUT == 0 and TM_OUT % (16 * 16) == 0
    assert tuple(d for _, d in DILATED_PATTERNS) == (1, 4, 16)
    for window, dilation in DILATED_PATTERNS:
        assert window == dilation * A_BLOCK and S % (dilation * A_BLOCK) == 0 and TM_PROJ % (16 * dilation) == 0
    bd, tril, expand = _constants()
    n_gate0 = 2 * M_QK_WIDTH + 3 * M_WIDTH
    x2d = x.reshape(B * S, D)
    for l in range(depth):
        w = w_in[l]
        w_all = jnp.concatenate(
            [w[:, :n_gate0], w[:, n_gate0 + 2 * M_HEADS:],
             jnp.pad(w[:, n_gate0:n_gate0 + 2 * M_HEADS], ((0, 0), (0, GATE_PAD - 2 * M_HEADS)))],
            axis=1).astype(BF16)
        gq = (jnp.tile(q_norm_g[l], A_HEADS) * (A_HEAD_DIM ** -0.5 * LOG2E))[None, :]
        gk = jnp.tile(k_norm_g[l], A_HEADS)[None, :]
        gb = jnp.pad(gate_b[l], (0, GATE_PAD - 2 * M_HEADS))[None, :]
        mq, mv, mo, mz, az, gates, mkt, *qkv_views = _inproj(x2d, norm_g[l][None, :], w_all, bd, gq, gk,
                                                             conv_w[l], conv_b[l][None, :], gb, tril, S)
        hm = _mlstm(mq, mkt, mv, mo, mz, gates, m_norm_g[l][None, :], B, S)
        outs, lses = [], []
        for n, (_, dilation) in enumerate(DILATED_PATTERNS):
            o, lse = _attn_pattern(*qkv_views[3 * n:3 * n + 3], B, S, dilation)
            outs.append(o)
            lses.append(lse)
        x2d = _outproj(x2d, hm, outs, lses, az, w_out[l].astype(BF16), expand)
    return x2d.reshape(B, S, D)
```
